```python
import math
import jax
import jax.numpy as jnp
from jax import lax
import numpy as np

D_MODEL = 1024
BATCH = 4
SEQ = 4096
DEPTH = 4

CTX_LEN = 256
GRID_W = 64
N_MIXERS = 2
N_MOD = 9
D_FF = 2816
S5_GROUP = 16
S5_GROUPS = D_MODEL // S5_GROUP
S5_STATE = 64
GDN_HEADS = 8
GDN_DK = D_MODEL // GDN_HEADS
GDN_DV = D_MODEL // GDN_HEADS
GDN_CONV = 5
GDN_CHUNK = 64
EPS = 1e-6
N_LAYERS_A = (DEPTH + 1) // 2
N_LAYERS_B = DEPTH // 2

kernel_name = "hybrid_s5_gdn_macaron_prefix_dit"


def rms_norm(x, g):
    xf = x.astype(jnp.float32)
    return xf * lax.rsqrt(jnp.mean(xf * xf, axis=-1, keepdims=True) + EPS) * g.astype(jnp.float32)


def ada_norm(x, g, shift, scale):
    return rms_norm(x, g) * (1.0 + scale) + shift


def swiglu(h, w13, w2):
    a, b = jnp.split(h @ w13, 2, axis=-1)
    return (jax.nn.silu(a) * b) @ w2


def to_col_major(h, rows):
    b, n, d = h.shape
    return h.reshape(b, rows, GRID_W, d).transpose(0, 2, 1, 3).reshape(b, n, d)


def to_row_major(h, rows):
    b, n, d = h.shape
    return h.reshape(b, GRID_W, rows, d).transpose(0, 2, 1, 3).reshape(b, n, d)


def cmul(ar, ai, br, bi):
    return ar * br - ai * bi, ar * bi + ai * br


def s5_combine(e1, e2):
    a1r, a1i, b1r, b1i = e1
    a2r, a2i, b2r, b2i = e2
    ar, ai = cmul(a2r, a2i, a1r, a1i)
    br, bi = cmul(a2r, a2i, b1r, b1i)
    return ar, ai, br + b2r, bi + b2i


def s5_states(u, a_re, a_im, log_dt, b_re, b_im):
    f32 = jnp.float32
    a_re = a_re.astype(f32)
    a_im = a_im.astype(f32)
    dt = jnp.exp(log_dt.astype(f32))[:, None]
    mag = jnp.exp(dt * a_re)
    ang = dt * a_im
    abar_re, abar_im = mag * jnp.cos(ang), mag * jnp.sin(ang)
    den = a_re * a_re + a_im * a_im
    n_re = abar_re - 1.0
    f_re = (n_re * a_re + abar_im * a_im) / den
    f_im = (abar_im * a_re - n_re * a_im) / den
    bu_re = jnp.einsum('btgc,gpc->btgp', u, b_re.astype(f32))
    bu_im = jnp.einsum('btgc,gpc->btgp', u, b_im.astype(f32))
    x_re, x_im = cmul(f_re, f_im, bu_re, bu_im)
    a_shape = (1, u.shape[1]) + abar_re.shape
    _, _, h_re, h_im = lax.associative_scan(
        s5_combine,
        (jnp.broadcast_to(abar_re, a_shape), jnp.broadcast_to(abar_im, a_shape), x_re, x_im),
        axis=1)
    return h_re, h_im


def s5_mixer(hx, hc, a_re, a_im, log_dt, b_re, b_im, c_re, c_im, d, glu_w, glu_b, with_ctx):
    bsz, seq_len, _ = hx.shape
    ctx_len = hc.shape[1]
    ux = hx.reshape(bsz, seq_len, S5_GROUPS, S5_GROUP)
    uc = hc.reshape(bsz, ctx_len, S5_GROUPS, S5_GROUP)
    yx = d * hx
    yc = d * hc
    for di in range(2):
        if di == 0:
            u = jnp.concatenate([uc, ux], axis=1)
        else:
            u = jnp.concatenate([jnp.flip(uc, 1), jnp.flip(ux, 1)], axis=1)
        h_re, h_im = s5_states(u, a_re[di], a_im[di], log_dt[di], b_re[di], b_im[di])

        def readout(hr, hi):
            y = (jnp.einsum('btgp,gcp->btgc', hr, c_re[di].astype(jnp.float32))
                 - jnp.einsum('btgp,gcp->btgc', hi, c_im[di].astype(jnp.float32)))
            y = y.reshape(bsz, -1, D_MODEL)
            return jnp.flip(y, 1) if di == 1 else y

        yx = yx + readout(h_re[:, ctx_len:], h_im[:, ctx_len:])
        if with_ctx:
            yc = yc + readout(h_re[:, :ctx_len], h_im[:, :ctx_len])

    def glu(y):
        z = jax.nn.gelu(y)
        p = z @ glu_w + glu_b
        return p[..., :D_MODEL] * jax.nn.sigmoid(p[..., D_MODEL:])

    return glu(yx), (glu(yc) if with_ctx else None)


def short_conv(u, w):
    pad = GDN_CONV // 2
    return lax.conv_general_dilated(
        u, w[:, None, :].astype(u.dtype), window_strides=(1,), padding=[(pad, pad)],
        dimension_numbers=('NWC', 'WIO', 'NWC'), feature_group_count=u.shape[-1])


def l2_normalise(t):
    tf = t.astype(jnp.float32)
    return tf * lax.rsqrt(jnp.sum(tf * tf, axis=-1, keepdims=True) + EPS)


def gdn_branch(p, conv_w, a_log, dt_bias):
    f32 = jnp.float32
    b, n, _ = p.shape
    qkv = jax.nn.silu(short_conv(p[..., :3 * D_MODEL], conv_w))
    q = l2_normalise(qkv[..., :D_MODEL].reshape(b, n, GDN_HEADS, GDN_DK)) * (GDN_DK ** -0.5)
    k = l2_normalise(qkv[..., D_MODEL:2 * D_MODEL].reshape(b, n, GDN_HEADS, GDN_DK))
    v = qkv[..., 2 * D_MODEL:].reshape(b, n, GDN_HEADS, GDN_DV).astype(f32)
    z = p[..., 3 * D_MODEL:4 * D_MODEL]
    o0 = 4 * D_MODEL
    beta = jax.nn.sigmoid(p[..., o0:o0 + 2 * GDN_HEADS].astype(f32)).reshape(b, n, 2, GDN_HEADS)
    alpha = p[..., o0 + 2 * GDN_HEADS:].astype(f32).reshape(b, n, 2, GDN_HEADS)
    g = -jnp.exp(a_log.astype(f32)) * jax.nn.softplus(alpha + dt_bias.astype(f32))
    return q, k, v, z, beta, g


def gated_delta_chunked(q, k, v, g, beta):
    bsz, n_pos, nh, dk = q.shape
    dv = v.shape[-1]
    nc = n_pos // GDN_CHUNK

    def chunks(t):
        return t.reshape(bsz, nc, GDN_CHUNK, nh, -1).transpose(0, 3, 1, 2, 4)

    q, k, v = chunks(q), chunks(k), chunks(v)
    g = g.reshape(bsz, nc, GDN_CHUNK, nh).transpose(0, 3, 1, 2)
    beta = beta.reshape(bsz, nc, GDN_CHUNK, nh).transpose(0, 3, 1, 2)
    gcum = jnp.cumsum(g, axis=-1)
    incl = jnp.tril(jnp.ones((GDN_CHUNK, GDN_CHUNK), dtype=bool))
    strict = jnp.tril(jnp.ones((GDN_CHUNK, GDN_CHUNK), dtype=bool), k=-1)
    decay = jnp.exp(jnp.where(incl, gcum[..., :, None] - gcum[..., None, :], -jnp.inf))
    k_beta = k * beta[..., None]
    a_mat = jnp.where(strict, jnp.einsum('bhncd,bhnsd->bhncs', k_beta, k) * decay, 0.0)
    lhs = a_mat + jnp.eye(GDN_CHUNK, dtype=a_mat.dtype)
    u_val = lax.linalg.triangular_solve(lhs, v * beta[..., None], left_side=True, lower=True)
    w_key = lax.linalg.triangular_solve(lhs, k_beta * jnp.exp(gcum)[..., None], left_side=True, lower=True)
    attn = jnp.where(incl, jnp.einsum('bhncd,bhnsd->bhncs', q, k) * decay, 0.0)
    q_dec = q * jnp.exp(gcum)[..., None]
    k_dec = k * jnp.exp(gcum[..., -1:] - gcum)[..., None]
    g_last = jnp.exp(gcum[..., -1])

    def step(state, inp):
        qd, kd, uv, wk, at, gl = inp
        v_new = uv - jnp.einsum('bhcd,bhde->bhce', wk, state)
        o = jnp.einsum('bhcd,bhde->bhce', qd, state) + jnp.einsum('bhcs,bhse->bhce', at, v_new)
        state = state * gl[..., None, None] + jnp.einsum('bhcd,bhce->bhde', kd, v_new)
        return state, o

    xs = tuple(jnp.moveaxis(t, 2, 0) for t in (q_dec, k_dec, u_val, w_key, attn, g_last))
    s0 = jnp.zeros((bsz, nh, dk, dv), jnp.float32)
    _, o = lax.scan(step, s0, xs)
    return o.transpose(1, 0, 3, 2, 4).reshape(bsz, n_pos, nh, dv)


def gdn_mixer(hx, hc, rows, w_in, conv_w, a_log, dt_bias, norm_g, w_out, with_ctx):
    ctx_len = hc.shape[1]
    qx, kx, vx, zx, bx, gx = gdn_branch(to_col_major(hx, rows) @ w_in, conv_w, a_log, dt_bias)
    qc, kc, vc, zc, bc, gc = gdn_branch(hc @ w_in, conv_w, a_log, dt_bias)
    ox, oc = None, None
    for di in range(2):
        def cat(tc, tx):
            if di == 0:
                return jnp.concatenate([tc, tx], axis=1)
            return jnp.concatenate([jnp.flip(tc, 1), jnp.flip(tx, 1)], axis=1)

        o = gated_delta_chunked(cat(qc, qx), cat(kc, kx), cat(vc, vx),
                                cat(gc[:, :, di], gx[:, :, di]), cat(bc[:, :, di], bx[:, :, di]))
        o_c, o_x = o[:, :ctx_len], o[:, ctx_len:]
        if di == 1:
            o_c, o_x = jnp.flip(o_c, 1), jnp.flip(o_x, 1)
        ox = o_x if ox is None else ox + o_x
        if with_ctx:
            oc = o_c if oc is None else oc + o_c

    def project(o, z):
        b, n = o.shape[:2]
        y = rms_norm(o, norm_g) * jax.nn.silu(z.reshape(b, n, GDN_HEADS, GDN_DV))
        return y.reshape(b, n, D_MODEL) @ w_out

    yx = to_row_major(project(ox, zx), rows)
    yc = project(oc, zc) if with_ctx else None
    return yx, yc


def setup_inputs(seed: int = 0) -> dict:
    key = jax.random.key(seed)
    ks = jax.random.split(key, 32)
    f32 = jnp.float32
    H = GDN_HEADS

    def nrm(k, shape, scale):
        return jax.random.normal(k, shape, f32) * scale

    x = nrm(ks[0], (BATCH, SEQ, D_MODEL), 1.0)
    c = nrm(ks[1], (BATCH, D_MODEL), 1.0)
    ctx = nrm(ks[2], (BATCH, CTX_LEN, D_MODEL), 1.0)
    c_ctx = nrm(ks[3], (D_MODEL,), 1.0)
    mod_w = nrm(ks[4], (DEPTH, D_MODEL, N_MOD * D_MODEL), 0.5 * D_MODEL ** -0.5)
    mod_b = nrm(ks[5], (DEPTH, N_MOD * D_MODEL), 0.01)
    norm_g = 1.0 + nrm(ks[6], (DEPTH, 3, D_MODEL), 0.01)
    ffn_w13 = nrm(ks[7], (DEPTH, 2, D_MODEL, 2 * D_FF), D_MODEL ** -0.5)
    ffn_w2 = nrm(ks[8], (DEPTH, 2, D_FF, D_MODEL), D_FF ** -0.5)
    na = N_LAYERS_A
    s5_a_re = -0.5 + nrm(ks[9], (na, 2, S5_GROUPS, S5_STATE), 0.01)
    s5_a_im = jnp.pi * jnp.arange(S5_STATE, dtype=f32) + nrm(ks[10], (na, 2, S5_GROUPS, S5_STATE), 0.01)
    s5_log_dt = jax.random.uniform(ks[11], (na, 2, S5_GROUPS), f32, math.log(1e-3), math.log(1e-1))
    s5_b_re = nrm(ks[12], (na, 2, S5_GROUPS, S5_STATE, S5_GROUP), (2 * S5_GROUP) ** -0.5)
    s5_b_im = nrm(ks[13], (na, 2, S5_GROUPS, S5_STATE, S5_GROUP), (2 * S5_GROUP) ** -0.5)
    s5_c_re = nrm(ks[14], (na, 2, S5_GROUPS, S5_GROUP, S5_STATE), S5_STATE ** -0.5)
    s5_c_im = nrm(ks[15], (na, 2, S5_GROUPS, S5_GROUP, S5_STATE), S5_STATE ** -0.5)
    s5_d = nrm(ks[16], (na, D_MODEL), 1.0)
    s5_glu_w = nrm(ks[17], (na, D_MODEL, 2 * D_MODEL), D_MODEL ** -0.5)
    s5_glu_b = nrm(ks[18], (na, 2 * D_MODEL), 0.01)
    nb = N_LAYERS_B
    gdn_w_in = nrm(ks[19], (nb, D_MODEL, 4 * D_MODEL + 4 * H), D_MODEL ** -0.5)
    gdn_conv_w = nrm(ks[20], (nb, GDN_CONV, 3 * D_MODEL), GDN_CONV ** -0.5)
    gdn_a_log = jnp.log(jax.random.uniform(ks[21], (nb, 2, H), f32, 1.0, 16.0))
    dt = jnp.exp(jax.random.uniform(ks[22], (nb, 2, H), f32, math.log(1e-3), math.log(1e-1)))
    gdn_dt_bias = dt + jnp.log(-jnp.expm1(-dt))
    gdn_norm_g = 1.0 + nrm(ks[23], (nb, GDN_DV), 0.01)
    gdn_w_out = nrm(ks[24], (nb, D_MODEL, D_MODEL), D_MODEL ** -0.5)
    final_g = 1.0 + nrm(ks[25], (D_MODEL,), 0.01)
    return {"x": x, "c": c, "ctx": ctx, "c_ctx": c_ctx,
            "mod_w": mod_w, "mod_b": mod_b, "norm_g": norm_g,
            "ffn_w13": ffn_w13, "ffn_w2": ffn_w2,
            "s5_a_re": s5_a_re, "s5_a_im": s5_a_im, "s5_log_dt": s5_log_dt,
            "s5_b_re": s5_b_re, "s5_b_im": s5_b_im, "s5_c_re": s5_c_re, "s5_c_im": s5_c_im,
            "s5_d": s5_d, "s5_glu_w": s5_glu_w, "s5_glu_b": s5_glu_b,
            "gdn_w_in": gdn_w_in, "gdn_conv_w": gdn_conv_w, "gdn_a_log": gdn_a_log,
            "gdn_dt_bias": gdn_dt_bias, "gdn_norm_g": gdn_norm_g, "gdn_w_out": gdn_w_out,
            "final_g": final_g}


def reference(x, c, ctx, c_ctx, mod_w, mod_b, norm_g, ffn_w13, ffn_w2,
              s5_a_re, s5_a_im, s5_log_dt, s5_b_re, s5_b_im, s5_c_re, s5_c_im,
              s5_d, s5_glu_w, s5_glu_b,
              gdn_w_in, gdn_conv_w, gdn_a_log, gdn_dt_bias, gdn_norm_g, gdn_w_out,
              final_g):
    bsz, seq_len, _ = x.shape
    rows = seq_len // GRID_W
    cx = ctx
    for i in range(DEPTH):
        last = i == DEPTH - 1
        mx = (jax.nn.silu(c) @ mod_w[i] + mod_b[i]).reshape(bsz, N_MOD, 1, D_MODEL)
        mc = (jax.nn.silu(c_ctx) @ mod_w[i] + mod_b[i]).reshape(N_MOD, 1, 1, D_MODEL)
        x = x + 0.5 * mx[:, 2] * swiglu(ada_norm(x, norm_g[i, 0], mx[:, 0], mx[:, 1]), ffn_w13[i, 0], ffn_w2[i, 0])
        cx = cx + 0.5 * mc[2] * swiglu(ada_norm(cx, norm_g[i, 0], mc[0], mc[1]), ffn_w13[i, 0], ffn_w2[i, 0])
        hx = ada_norm(x, norm_g[i, 1], mx[:, 3], mx[:, 4])
        hc = ada_norm(cx, norm_g[i, 1], mc[3], mc[4])
        j = i // N_MIXERS
        if i % N_MIXERS == 0:
            ox, oc = s5_mixer(hx, hc, s5_a_re[j], s5_a_im[j], s5_log_dt[j], s5_b_re[j], s5_b_im[j],
                              s5_c_re[j], s5_c_im[j], s5_d[j], s5_glu_w[j], s5_glu_b[j], not last)
        else:
            ox, oc = gdn_mixer(hx, hc, rows, gdn_w_in[j], gdn_conv_w[j], gdn_a_log[j], gdn_dt_bias[j],
                               gdn_norm_g[j], gdn_w_out[j], not last)
        x = x + mx[:, 5] * ox
        x = x + 0.5 * mx[:, 8] * swiglu(ada_norm(x, norm_g[i, 2], mx[:, 6], mx[:, 7]), ffn_w13[i, 1], ffn_w2[i, 1])
        if not last:
            cx = cx + mc[5] * oc
            cx = cx + 0.5 * mc[8] * swiglu(ada_norm(cx, norm_g[i, 2], mc[6], mc[7]), ffn_w13[i, 1], ffn_w2[i, 1])
    return rms_norm(x, final_g)
```

```python
import functools

import jax
import jax.numpy as jnp
from jax import lax
from jax.experimental import pallas as pl
from jax.experimental.pallas import tpu as pltpu

D = 1024
B = 4
SEQ = 4096
CTX = 256
T = SEQ + CTX
DEPTH = 4
GRID_W = 64
N_MOD = 9
D_FF = 2816
S5_GROUP = 16
S5_GROUPS = D // S5_GROUP
S5_STATE = 64
GDN_HEADS = 8
GDN_DK = D // GDN_HEADS
GDN_DV = D // GDN_HEADS
GDN_CONV = 5
GDN_CHUNK = 64
EPS = 1e-6

MOD_ROWS = 8
V7X_VMEM_LIMIT = 56 * 1024 * 1024

f32 = jnp.float32
bf16 = jnp.bfloat16


MOD_TN = 2304


def _mod_kernel(c_ref, w_ref, b_ref, o_ref):
    c = c_ref[...]
    s = c * jax.nn.sigmoid(c)
    o_ref[0] = jnp.dot(s, w_ref[0], preferred_element_type=f32,
                       precision=lax.Precision.HIGHEST) + b_ref[0]


def _modulation(c, c_ctx, mod_w, mod_b):
    cc = jnp.zeros((MOD_ROWS, D), f32).at[:B].set(c).at[B].set(c_ctx)
    n = N_MOD * D
    return pl.pallas_call(
        _mod_kernel,
        grid=(DEPTH, n // MOD_TN),
        in_specs=[
            pl.BlockSpec((MOD_ROWS, D), lambda i, j: (0, 0)),
            pl.BlockSpec((1, D, MOD_TN), lambda i, j: (i, 0, j)),
            pl.BlockSpec((1, 1, MOD_TN), lambda i, j: (i, 0, j)),
        ],
        out_specs=pl.BlockSpec((1, MOD_ROWS, MOD_TN), lambda i, j: (i, 0, j)),
        out_shape=jax.ShapeDtypeStruct((DEPTH, MOD_ROWS, n), f32),
        compiler_params=pltpu.CompilerParams(
            dimension_semantics=("arbitrary", "arbitrary"), vmem_limit_bytes=V7X_VMEM_LIMIT),
        name="modulation",
    )(cc, mod_w, mod_b.reshape(DEPTH, 1, n))


def _row_modulation(m_ref, b, t, tm):
    rows = t * tm + lax.broadcasted_iota(jnp.int32, (tm, 1), 0)
    is_ctx = rows >= SEQ
    mx = m_ref[b]
    mc = m_ref[B]
    return tuple(jnp.where(is_ctx, mc[j:j + 1], mx[j:j + 1]) for j in range(3))


def _ada_norm(x, g, shift, scale):
    ms = jnp.mean(x * x, axis=-1, keepdims=True)
    return x * lax.rsqrt(ms + EPS) * g * (1.0 + scale) + shift


FFN_TM = 544
FFN_CHUNK = 256


def _ffn_kernel(m_ref, g_ref, x_ref, w13_ref, w2_ref, o_ref):
    b = pl.program_id(0)
    t = pl.program_id(1)
    x = x_ref[0]
    shift, scale, gate = _row_modulation(m_ref, b, t, FFN_TM)
    h = _ada_norm(x, g_ref[...], shift, scale).astype(bf16)
    acc = jnp.zeros((FFN_TM, D), f32)
    for j in range(D_FF // FFN_CHUNK):
        lo = j * FFN_CHUNK
        a = jnp.dot(h, w13_ref[:, lo:lo + FFN_CHUNK], preferred_element_type=f32)
        u = jnp.dot(h, w13_ref[:, D_FF + lo:D_FF + lo + FFN_CHUNK], preferred_element_type=f32)
        hid = (a * jax.nn.sigmoid(a) * u).astype(bf16)
        acc = acc + jnp.dot(hid, w2_ref[lo:lo + FFN_CHUNK, :], preferred_element_type=f32)
    o_ref[0] = x + 0.5 * gate * acc


def _ffn(tok, m3, g, w13, w2):
    return pl.pallas_call(
        _ffn_kernel,
        grid=(B, T // FFN_TM),
        in_specs=[
            pl.BlockSpec((MOD_ROWS, 3, D), lambda b, t: (0, 0, 0)),
            pl.BlockSpec((1, D), lambda b, t: (0, 0)),
            pl.BlockSpec((1, FFN_TM, D), lambda b, t: (b, t, 0)),
            pl.BlockSpec((D, 2 * D_FF), lambda b, t: (0, 0), pipeline_mode=pl.Buffered(1)),
            pl.BlockSpec((D_FF, D), lambda b, t: (0, 0), pipeline_mode=pl.Buffered(1)),
        ],
        out_specs=pl.BlockSpec((1, FFN_TM, D), lambda b, t: (b, t, 0)),
        out_shape=jax.ShapeDtypeStruct((B, T, D), f32),
        compiler_params=pltpu.CompilerParams(
            dimension_semantics=("arbitrary", "arbitrary"), vmem_limit_bytes=V7X_VMEM_LIMIT),
        name="ffn",
    )(m3, g, tok, w13, w2)


NORM_TM = 544


def _norm_kernel(m_ref, g_ref, x_ref, o_ref):
    b = pl.program_id(0)
    t = pl.program_id(1)
    shift, scale, _ = _row_modulation(m_ref, b, t, NORM_TM)
    o_ref[0] = _ada_norm(x_ref[0], g_ref[...], shift, scale)


def _mixer_norm(tok, m3, g):
    return pl.pallas_call(
        _norm_kernel,
        grid=(B, T // NORM_TM),
        in_specs=[
            pl.BlockSpec((MOD_ROWS, 3, D), lambda b, t: (0, 0, 0)),
            pl.BlockSpec((1, D), lambda b, t: (0, 0)),
            pl.BlockSpec((1, NORM_TM, D), lambda b, t: (b, t, 0)),
        ],
        out_specs=pl.BlockSpec((1, NORM_TM, D), lambda b, t: (b, t, 0)),
        out_shape=jax.ShapeDtypeStruct((B, T, D), f32),
        compiler_params=pltpu.CompilerParams(dimension_semantics=("arbitrary", "arbitrary")),
        name="mixer_norm",
    )(m3, g, tok)


FINAL_TM = 512


def _final_kernel(g_ref, x_ref, o_ref):
    x = x_ref[0]
    ms = jnp.mean(x * x, axis=-1, keepdims=True)
    o_ref[0] = x * lax.rsqrt(ms + EPS) * g_ref[...]


def _final_norm(tok, g):
    return pl.pallas_call(
        _final_kernel,
        grid=(B, SEQ // FINAL_TM),
        in_specs=[
            pl.BlockSpec((1, D), lambda b, t: (0, 0)),
            pl.BlockSpec((1, FINAL_TM, D), lambda b, t: (b, t, 0)),
        ],
        out_specs=pl.BlockSpec((1, FINAL_TM, D), lambda b, t: (b, t, 0)),
        out_shape=jax.ShapeDtypeStruct((B, SEQ, D), f32),
        compiler_params=pltpu.CompilerParams(dimension_semantics=("arbitrary", "arbitrary")),
        name="final_norm",
    )(g, tok)


def _cmul(ar, ai, br, bi):
    return ar * br - ai * bi, ar * bi + ai * br


def _s5_combine(e1, e2):
    a1r, a1i, b1r, b1i = e1
    a2r, a2i, b2r, b2i = e2
    ar, ai = _cmul(a2r, a2i, a1r, a1i)
    br, bi = _cmul(a2r, a2i, b1r, b1i)
    return ar, ai, br + b2r, bi + b2i


def _s5_states(u, a_re, a_im, log_dt, b_re, b_im):
    dt = jnp.exp(log_dt)[:, None]
    mag = jnp.exp(dt * a_re)
    ang = dt * a_im
    abar_re, abar_im = mag * jnp.cos(ang), mag * jnp.sin(ang)
    den = a_re * a_re + a_im * a_im
    n_re = abar_re - 1.0
    f_re = (n_re * a_re + abar_im * a_im) / den
    f_im = (abar_im * a_re - n_re * a_im) / den
    bu_re = jnp.einsum('btgc,gpc->btgp', u, b_re)
    bu_im = jnp.einsum('btgc,gpc->btgp', u, b_im)
    x_re, x_im = _cmul(f_re, f_im, bu_re, bu_im)
    a_shape = (1, u.shape[1]) + abar_re.shape
    _, _, h_re, h_im = lax.associative_scan(
        _s5_combine,
        (jnp.broadcast_to(abar_re, a_shape), jnp.broadcast_to(abar_im, a_shape), x_re, x_im),
        axis=1)
    return h_re, h_im


def _s5_mixer_jax(hx, hc, a_re, a_im, log_dt, b_re, b_im, c_re, c_im, d, glu_w, glu_b):
    bsz, seq_len, _ = hx.shape
    ctx_len = hc.shape[1]
    ux = hx.reshape(bsz, seq_len, S5_GROUPS, S5_GROUP)
    uc = hc.reshape(bsz, ctx_len, S5_GROUPS, S5_GROUP)
    yx = d * hx
    yc = d * hc
    for di in range(2):
        if di == 0:
            u = jnp.concatenate([uc, ux], axis=1)
        else:
            u = jnp.concatenate([jnp.flip(uc, 1), jnp.flip(ux, 1)], axis=1)
        h_re, h_im = _s5_states(u, a_re[di], a_im[di], log_dt[di], b_re[di], b_im[di])

        def readout(hr, hi):
            y = (jnp.einsum('btgp,gcp->btgc', hr, c_re[di]) - jnp.einsum('btgp,gcp->btgc', hi, c_im[di]))
            y = y.reshape(bsz, -1, D)
            return jnp.flip(y, 1) if di == 1 else y

        yx = yx + readout(h_re[:, ctx_len:], h_im[:, ctx_len:])
        yc = yc + readout(h_re[:, :ctx_len], h_im[:, :ctx_len])

    def glu(y):
        z = jax.nn.gelu(y)
        p = z @ glu_w + glu_b
        return p[..., :D] * jax.nn.sigmoid(p[..., D:])

    return glu(yx), glu(yc)


def _to_col_major(h, rows):
    b, n, d = h.shape
    return h.reshape(b, rows, GRID_W, d).transpose(0, 2, 1, 3).reshape(b, n, d)


def _to_row_major(h, rows):
    b, n, d = h.shape
    return h.reshape(b, GRID_W, rows, d).transpose(0, 2, 1, 3).reshape(b, n, d)


def _short_conv(u, w):
    pad = GDN_CONV // 2
    return lax.conv_general_dilated(
        u, w[:, None, :].astype(u.dtype), window_strides=(1,), padding=[(pad, pad)],
        dimension_numbers=('NWC', 'WIO', 'NWC'), feature_group_count=u.shape[-1])


def _l2n(t):
    return t * lax.rsqrt(jnp.sum(t * t, axis=-1, keepdims=True) + EPS)


def _gdn_branch(p, conv_w, a_log, dt_bias):
    b, n, _ = p.shape
    qkv = jax.nn.silu(_short_conv(p[..., :3 * D], conv_w))
    q = _l2n(qkv[..., :D].reshape(b, n, GDN_HEADS, GDN_DK)) * (GDN_DK ** -0.5)
    k = _l2n(qkv[..., D:2 * D].reshape(b, n, GDN_HEADS, GDN_DK))
    v = qkv[..., 2 * D:].reshape(b, n, GDN_HEADS, GDN_DV)
    z = p[..., 3 * D:4 * D]
    o0 = 4 * D
    beta = jax.nn.sigmoid(p[..., o0:o0 + 2 * GDN_HEADS]).reshape(b, n, 2, GDN_HEADS)
    alpha = p[..., o0 + 2 * GDN_HEADS:].reshape(b, n, 2, GDN_HEADS)
    g = -jnp.exp(a_log) * jax.nn.softplus(alpha + dt_bias)
    return q, k, v, z, beta, g


def _gated_delta_chunked(q, k, v, g, beta):
    bsz, n_pos, nh, dk = q.shape
    dv = v.shape[-1]
    nc = n_pos // GDN_CHUNK

    def chunks(t):
        return t.reshape(bsz, nc, GDN_CHUNK, nh, -1).transpose(0, 3, 1, 2, 4)

    q, k, v = chunks(q), chunks(k), chunks(v)
    g = g.reshape(bsz, nc, GDN_CHUNK, nh).transpose(0, 3, 1, 2)
    beta = beta.reshape(bsz, nc, GDN_CHUNK, nh).transpose(0, 3, 1, 2)
    gcum = jnp.cumsum(g, axis=-1)
    incl = jnp.tril(jnp.ones((GDN_CHUNK, GDN_CHUNK), dtype=bool))
    strict = jnp.tril(jnp.ones((GDN_CHUNK, GDN_CHUNK), dtype=bool), k=-1)
    decay = jnp.exp(jnp.where(incl, gcum[..., :, None] - gcum[..., None, :], -jnp.inf))
    k_beta = k * beta[..., None]
    a_mat = jnp.where(strict, jnp.einsum('bhncd,bhnsd->bhncs', k_beta, k) * decay, 0.0)
    lhs = a_mat + jnp.eye(GDN_CHUNK, dtype=a_mat.dtype)
    u_val = lax.linalg.triangular_solve(lhs, v * beta[..., None], left_side=True, lower=True)
    w_key = lax.linalg.triangular_solve(lhs, k_beta * jnp.exp(gcum)[..., None], left_side=True, lower=True)
    attn = jnp.where(incl, jnp.einsum('bhncd,bhnsd->bhncs', q, k) * decay, 0.0)
    q_dec = q * jnp.exp(gcum)[..., None]
    k_dec = k * jnp.exp(gcum[..., -1:] - gcum)[..., None]
    g_last = jnp.exp(gcum[..., -1])

    def step(state, inp):
        qd, kd, uv, wk, at, gl = inp
        v_new = uv - jnp.einsum('bhcd,bhde->bhce', wk, state)
        o = jnp.einsum('bhcd,bhde->bhce', qd, state) + jnp.einsum('bhcs,bhse->bhce', at, v_new)
        state = state * gl[..., None, None] + jnp.einsum('bhcd,bhce->bhde', kd, v_new)
        return state, o

    xs = tuple(jnp.moveaxis(t, 2, 0) for t in (q_dec, k_dec, u_val, w_key, attn, g_last))
    s0 = jnp.zeros((bsz, nh, dk, dv), f32)
    _, o = lax.scan(step, s0, xs)
    return o.transpose(1, 0, 3, 2, 4).reshape(bsz, n_pos, nh, dv)


def _gdn_mixer_jax(hx, hc, rows, w_in, conv_w, a_log, dt_bias, norm_g, w_out):
    ctx_len = hc.shape[1]
    qx, kx, vx, zx, bx, gx = _gdn_branch(_to_col_major(hx, rows) @ w_in, conv_w, a_log, dt_bias)
    qc, kc, vc, zc, bc, gc = _gdn_branch(hc @ w_in, conv_w, a_log, dt_bias)
    ox, oc = None, None
    for di in range(2):
        def cat(tc, tx):
            if di == 0:
                return jnp.concatenate([tc, tx], axis=1)
            return jnp.concatenate([jnp.flip(tc, 1), jnp.flip(tx, 1)], axis=1)

        o = _gated_delta_chunked(cat(qc, qx), cat(kc, kx), cat(vc, vx),
                                 cat(gc[:, :, di], gx[:, :, di]), cat(bc[:, :, di], bx[:, :, di]))
        o_c, o_x = o[:, :ctx_len], o[:, ctx_len:]
        if di == 1:
            o_c, o_x = jnp.flip(o_c, 1), jnp.flip(o_x, 1)
        ox = o_x if ox is None else ox + o_x
        oc = o_c if oc is None else oc + o_c

    def project(o, z):
        b, n = o.shape[:2]
        of = o
        y = (of * lax.rsqrt(jnp.mean(of * of, axis=-1, keepdims=True) + EPS) * norm_g
             * jax.nn.silu(z.reshape(b, n, GDN_HEADS, GDN_DV)))
        return y.reshape(b, n, D) @ w_out

    yx = _to_row_major(project(ox, zx), rows)
    yc = project(oc, zc)
    return yx, yc


def kernel(x, c, ctx, c_ctx, mod_w, mod_b, norm_g, ffn_w13, ffn_w2, s5_a_re, s5_a_im, s5_log_dt,
           s5_b_re, s5_b_im, s5_c_re, s5_c_im, s5_d, s5_glu_w, s5_glu_b, gdn_w_in, gdn_conv_w,
           gdn_a_log, gdn_dt_bias, gdn_norm_g, gdn_w_out, final_g):
    tok = jnp.concatenate([x, ctx], axis=1)
    mods = _modulation(c, c_ctx, mod_w, mod_b).reshape(DEPTH, MOD_ROWS, N_MOD, D)
    w13 = ffn_w13.astype(bf16)
    w2 = ffn_w2.astype(bf16)
    rows = SEQ // GRID_W
    for i in range(DEPTH):
        m = mods[i]
        tok = _ffn(tok, m[:, 0:3], norm_g[i, 0][None], w13[i, 0], w2[i, 0])
        h = _mixer_norm(tok, m[:, 3:6], norm_g[i, 1][None])
        hx, hc = h[:, :SEQ], h[:, SEQ:]
        j = i // 2
        if i % 2 == 0:
            ox, oc = _s5_mixer_jax(hx, hc, s5_a_re[j], s5_a_im[j], s5_log_dt[j], s5_b_re[j], s5_b_im[j],
                                   s5_c_re[j], s5_c_im[j], s5_d[j], s5_glu_w[j], s5_glu_b[j])
        else:
            ox, oc = _gdn_mixer_jax(hx, hc, rows, gdn_w_in[j], gdn_conv_w[j], gdn_a_log[j], gdn_dt_bias[j],
                                    gdn_norm_g[j], gdn_w_out[j])
        gate_x = m[:B, 5][:, None, :]
        gate_c = m[B, 5][None, None, :]
        tok = tok + jnp.concatenate([gate_x * ox, gate_c * oc], axis=1)
        tok = _ffn(tok, m[:, 6:9], norm_g[i, 2][None], w13[i, 1], w2[i, 1])
    return _final_norm(tok, final_g[None])
```

```python
import functools

import jax
import jax.numpy as jnp
from jax import lax
from jax.experimental import pallas as pl
from jax.experimental.pallas import tpu as pltpu

D = 1024
B = 4
SEQ = 4096
CTX = 256
T = SEQ + CTX
DEPTH = 4
GRID_W = 64
N_MOD = 9
D_FF = 2816
S5_GROUP = 16
S5_GROUPS = D // S5_GROUP
S5_STATE = 64
GDN_HEADS = 8
GDN_DK = D // GDN_HEADS
GDN_DV = D // GDN_HEADS
GDN_CONV = 5
GDN_CHUNK = 64
EPS = 1e-6

MOD_ROWS = 8
V7X_VMEM_LIMIT = 56 * 1024 * 1024

f32 = jnp.float32
bf16 = jnp.bfloat16


MOD_TN = 2304


def _mod_kernel(c_ref, w_ref, b_ref, o_ref):
    c = c_ref[...]
    s = c * jax.nn.sigmoid(c)
    o_ref[0] = jnp.dot(s, w_ref[0], preferred_element_type=f32,
                       precision=lax.Precision.HIGHEST) + b_ref[0]


def _modulation(c, c_ctx, mod_w, mod_b):
    cc = jnp.zeros((MOD_ROWS, D), f32).at[:B].set(c).at[B].set(c_ctx)
    n = N_MOD * D
    return pl.pallas_call(
        _mod_kernel,
        grid=(DEPTH, n // MOD_TN),
        in_specs=[
            pl.BlockSpec((MOD_ROWS, D), lambda i, j: (0, 0)),
            pl.BlockSpec((1, D, MOD_TN), lambda i, j: (i, 0, j)),
            pl.BlockSpec((1, 1, MOD_TN), lambda i, j: (i, 0, j)),
        ],
        out_specs=pl.BlockSpec((1, MOD_ROWS, MOD_TN), lambda i, j: (i, 0, j)),
        out_shape=jax.ShapeDtypeStruct((DEPTH, MOD_ROWS, n), f32),
        compiler_params=pltpu.CompilerParams(
            dimension_semantics=("arbitrary", "arbitrary"), vmem_limit_bytes=V7X_VMEM_LIMIT),
        name="modulation",
    )(cc, mod_w, mod_b.reshape(DEPTH, 1, n))


def _row_modulation(m_ref, b, t, tm):
    rows = t * tm + lax.broadcasted_iota(jnp.int32, (tm, 1), 0)
    is_ctx = rows >= SEQ
    mx = m_ref[b]
    mc = m_ref[B]
    return tuple(jnp.where(is_ctx, mc[j:j + 1], mx[j:j + 1]) for j in range(3))


def _ada_norm(x, g, shift, scale):
    ms = jnp.mean(x * x, axis=-1, keepdims=True)
    return x * lax.rsqrt(ms + EPS) * g * (1.0 + scale) + shift


FFN_TM = 544
FFN_CHUNK = 256


def _ffn_kernel(m_ref, g_ref, x_ref, w13_ref, w2_ref, o_ref):
    b = pl.program_id(0)
    t = pl.program_id(1)
    x = x_ref[0]
    shift, scale, gate = _row_modulation(m_ref, b, t, FFN_TM)
    h = _ada_norm(x, g_ref[...], shift, scale).astype(bf16)
    acc = jnp.zeros((FFN_TM, D), f32)
    for j in range(D_FF // FFN_CHUNK):
        lo = j * FFN_CHUNK
        a = jnp.dot(h, w13_ref[:, lo:lo + FFN_CHUNK], preferred_element_type=f32)
        u = jnp.dot(h, w13_ref[:, D_FF + lo:D_FF + lo + FFN_CHUNK], preferred_element_type=f32)
        hid = (a * jax.nn.sigmoid(a) * u).astype(bf16)
        acc = acc + jnp.dot(hid, w2_ref[lo:lo + FFN_CHUNK, :], preferred_element_type=f32)
    o_ref[0] = x + 0.5 * gate * acc


def _ffn(tok, m3, g, w13, w2):
    return pl.pallas_call(
        _ffn_kernel,
        grid=(B, T // FFN_TM),
        in_specs=[
            pl.BlockSpec((MOD_ROWS, 3, D), lambda b, t: (0, 0, 0)),
            pl.BlockSpec((1, D), lambda b, t: (0, 0)),
            pl.BlockSpec((1, FFN_TM, D), lambda b, t: (b, t, 0)),
            pl.BlockSpec((D, 2 * D_FF), lambda b, t: (0, 0), pipeline_mode=pl.Buffered(1)),
            pl.BlockSpec((D_FF, D), lambda b, t: (0, 0), pipeline_mode=pl.Buffered(1)),
        ],
        out_specs=pl.BlockSpec((1, FFN_TM, D), lambda b, t: (b, t, 0)),
        out_shape=jax.ShapeDtypeStruct((B, T, D), f32),
        compiler_params=pltpu.CompilerParams(
            dimension_semantics=("arbitrary", "arbitrary"), vmem_limit_bytes=V7X_VMEM_LIMIT),
        name="ffn",
    )(m3, g, tok, w13, w2)


NORM_TM = 544


def _norm_kernel(m_ref, g_ref, x_ref, o_ref):
    b = pl.program_id(0)
    t = pl.program_id(1)
    shift, scale, _ = _row_modulation(m_ref, b, t, NORM_TM)
    o_ref[0] = _ada_norm(x_ref[0], g_ref[...], shift, scale)


def _mixer_norm(tok, m3, g):
    return pl.pallas_call(
        _norm_kernel,
        grid=(B, T // NORM_TM),
        in_specs=[
            pl.BlockSpec((MOD_ROWS, 3, D), lambda b, t: (0, 0, 0)),
            pl.BlockSpec((1, D), lambda b, t: (0, 0)),
            pl.BlockSpec((1, NORM_TM, D), lambda b, t: (b, t, 0)),
        ],
        out_specs=pl.BlockSpec((1, NORM_TM, D), lambda b, t: (b, t, 0)),
        out_shape=jax.ShapeDtypeStruct((B, T, D), f32),
        compiler_params=pltpu.CompilerParams(dimension_semantics=("arbitrary", "arbitrary")),
        name="mixer_norm",
    )(m3, g, tok)


FINAL_TM = 512


def _final_kernel(g_ref, x_ref, o_ref):
    x = x_ref[0]
    ms = jnp.mean(x * x, axis=-1, keepdims=True)
    o_ref[0] = x * lax.rsqrt(ms + EPS) * g_ref[...]


def _final_norm(tok, g):
    return pl.pallas_call(
        _final_kernel,
        grid=(B, SEQ // FINAL_TM),
        in_specs=[
            pl.BlockSpec((1, D), lambda b, t: (0, 0)),
            pl.BlockSpec((1, FINAL_TM, D), lambda b, t: (b, t, 0)),
        ],
        out_specs=pl.BlockSpec((1, FINAL_TM, D), lambda b, t: (b, t, 0)),
        out_shape=jax.ShapeDtypeStruct((B, SEQ, D), f32),
        compiler_params=pltpu.CompilerParams(dimension_semantics=("arbitrary", "arbitrary")),
        name="final_norm",
    )(g, tok)


S5_L = 64
S5_NCH = T // S5_L
S5_CCH = CTX // S5_L
S5_ROWS = S5_NCH * B
S5_W = S5_L * S5_GROUP
S5_BASE = 8


def _cmul(ar, ai, br, bi):
    return ar * br - ai * bi, ar * bi + ai * br


def _dot_nt(a, b):
    return lax.dot_general(a, b, (((1,), (1,)), ((), ())), preferred_element_type=f32)


def _s5_kernel(u_ref, are_ref, aim_ref, ldt_ref, btre_ref, btim_ref, cre_ref, cim_ref, dt_ref, y_ref,
               m_ref, s_ref, hf_ref, hb_ref):
    P = S5_STATE
    lane = lax.broadcasted_iota(jnp.int32, (1, 2 * P), 1)
    is_f = lane < P
    mf = is_f.astype(f32)
    mb = 1.0 - mf

    a_re = are_ref[0]
    a_im = aim_ref[0]
    dt = jnp.exp(ldt_ref[0])
    mag = jnp.exp(dt * a_re)
    ang = dt * a_im
    l_re = mag * jnp.cos(ang)
    l_im = mag * jnp.sin(ang)
    den = a_re * a_re + a_im * a_im
    n_re = l_re - 1.0
    f_re = (n_re * a_re + l_im * a_im) / den
    f_im = (l_im * a_re - n_re * a_im) / den

    c_re = cre_ref[0]
    c_im = cim_ref[0]
    a_r, a_i = _cmul(f_re, f_im, btre_ref[0], btim_ref[0])
    r_r, r_i = _cmul(l_re, l_im, c_re, c_im)
    p_re, p_im = l_re, l_im
    for _ in range(6):
        fm_re = jnp.where(is_f, p_re, 1.0)
        fm_im = jnp.where(is_f, p_im, 0.0)
        bm_re = jnp.where(is_f, 1.0, p_re)
        bm_im = jnp.where(is_f, 0.0, p_im)
        at_r, at_i = _cmul(fm_re, fm_im, a_r, a_i)
        ab_r, ab_i = _cmul(bm_re, bm_im, a_r, a_i)
        rt_r, rt_i = _cmul(bm_re, bm_im, r_r, r_i)
        rb_r, rb_i = _cmul(fm_re, fm_im, r_r, r_i)
        a_r = jnp.concatenate([at_r, ab_r], axis=0)
        a_i = jnp.concatenate([at_i, ab_i], axis=0)
        r_r = jnp.concatenate([rt_r, rb_r], axis=0)
        r_i = jnp.concatenate([rt_i, rb_i], axis=0)
        p_re, p_im = _cmul(p_re, p_im, p_re, p_im)
    acat = jnp.concatenate([a_r, a_i], axis=1).astype(bf16)
    rcat = jnp.concatenate([r_r, -r_i], axis=1)
    mf2 = jnp.concatenate([mf, mf], axis=1)
    rtf = (rcat * mf2).astype(bf16)
    rtb = (rcat * (1.0 - mf2)).astype(bf16)
    rcat = rcat.astype(bf16)

    nb = S5_BASE
    ccat = jnp.concatenate([c_re, -c_im], axis=1)
    ctf = jnp.concatenate([(ccat * mf2).astype(bf16)] * nb, axis=0)
    ctb = jnp.concatenate([(ccat * (1.0 - mf2)).astype(bf16)] * nb, axis=0)
    bw = nb * S5_GROUP
    kf_all = _dot_nt(acat[S5_W - bw:], ctf)
    kb_all = _dot_nt(acat[:bw], ctb)
    lane_t0 = lax.broadcasted_iota(jnp.int32, (S5_GROUP, bw), 1) // S5_GROUP
    lane_i = lax.broadcasted_iota(jnp.int32, (S5_GROUP, bw), 1)
    row_i = lax.broadcasted_iota(jnp.int32, (S5_GROUP, bw), 0)
    dtile = dt_ref[0]
    blocks = []
    for s0 in range(nb):
        acc = jnp.where(lane_i == s0 * S5_GROUP + row_i, dtile, 0.0)
        for lag in range(nb - s0):
            acc = acc + jnp.where(lane_t0 == s0 + lag, kf_all[(nb - 1 - lag) * S5_GROUP:(nb - lag) * S5_GROUP], 0.0)
        for lag in range(s0 + 1):
            acc = acc + jnp.where(lane_t0 == s0 - lag, kb_all[lag * S5_GROUP:(lag + 1) * S5_GROUP], 0.0)
        blocks.append(acc)
    blk0 = jnp.concatenate(blocks, axis=0).astype(bf16)
    for i in range(S5_L // nb):
        m_ref[i * bw:(i + 1) * bw, i * bw:(i + 1) * bw] = blk0
    n = nb
    while n < S5_L:
        w = n * S5_GROUP
        xn = _dot_nt(acat[S5_W - w:], rtf[:w]).astype(bf16)
        yn = _dot_nt(acat[:w], rtb[S5_W - w:]).astype(bf16)
        for j in range(S5_L // (2 * n)):
            o = 2 * j * w
            m_ref[o:o + w, o + w:o + 2 * w] = xn
            m_ref[o + w:o + 2 * w, o:o + w] = yn
        n *= 2

    u = u_ref[0]
    y = jnp.dot(u, m_ref[...], preferred_element_type=f32)
    s_ref[...] = jnp.dot(u, acat, preferred_element_type=f32)

    h_re = jnp.zeros((B, 2 * P), f32)
    h_im = jnp.zeros((B, 2 * P), f32)
    for j in range(S5_NCH):
        kf = j
        kb = S5_CCH - 1 - j if j < S5_CCH else S5_NCH + S5_CCH - 1 - j
        hf_ref[kf * B:(kf + 1) * B, :] = jnp.concatenate([h_re, h_im], axis=1)
        hb_ref[kb * B:(kb + 1) * B, :] = jnp.concatenate([h_re, h_im], axis=1)
        s_re = jnp.where(is_f, s_ref[kf * B:(kf + 1) * B, 0:2 * P], s_ref[kb * B:(kb + 1) * B, 0:2 * P])
        s_im = jnp.where(is_f, s_ref[kf * B:(kf + 1) * B, 2 * P:4 * P], s_ref[kb * B:(kb + 1) * B, 2 * P:4 * P])
        h_re, h_im = (p_re * h_re - p_im * h_im + s_re, p_re * h_im + p_im * h_re + s_im)
    hcat = jnp.where(mf2 > 0.5, hf_ref[...], hb_ref[...]).astype(bf16)
    y_ref[0] = y + _dot_nt(hcat, rcat)


def _s5_core(uflat, a_re, a_im, log_dt, b_re, b_im, c_re, c_im, d):
    G = S5_GROUPS

    def pack(t):
        return jnp.concatenate([t[0], t[1]], axis=-1)

    are = pack(a_re)[:, None, :]
    aim = pack(a_im)[:, None, :]
    ldt = pack(jnp.broadcast_to(log_dt[:, :, None], (2, G, S5_STATE)))[:, None, :]
    btre = pack(jnp.swapaxes(b_re, -1, -2))
    btim = pack(jnp.swapaxes(b_im, -1, -2))
    cre = pack(c_re)
    cim = pack(c_im)
    dtile = jnp.tile(d.reshape(G, 1, S5_GROUP), (1, 1, S5_BASE))
    vec = pl.BlockSpec((1, 1, 2 * S5_STATE), lambda g: (g, 0, 0))
    mat = pl.BlockSpec((1, S5_GROUP, 2 * S5_STATE), lambda g: (g, 0, 0))
    big = pl.BlockSpec((1, S5_ROWS, S5_W), lambda g: (g, 0, 0))
    return pl.pallas_call(
        _s5_kernel,
        grid=(G,),
        in_specs=[big, vec, vec, vec, mat, mat, mat, mat, vec],
        out_specs=big,
        out_shape=jax.ShapeDtypeStruct((G, S5_ROWS, S5_W), f32),
        scratch_shapes=[
            pltpu.VMEM((S5_W, S5_W), bf16),
            pltpu.VMEM((S5_ROWS, 4 * S5_STATE), f32),
            pltpu.VMEM((S5_ROWS, 4 * S5_STATE), f32),
            pltpu.VMEM((S5_ROWS, 4 * S5_STATE), f32),
        ],
        compiler_params=pltpu.CompilerParams(
            dimension_semantics=("arbitrary",), vmem_limit_bytes=V7X_VMEM_LIMIT),
        name="s5_core",
    )(uflat, are, aim, ldt, btre, btim, cre, cim, dtile)


def _s5_flatten(h):
    hs = jnp.concatenate([h[:, SEQ:], h[:, :SEQ]], axis=1)
    hs = hs.reshape(B, S5_NCH, S5_L, S5_GROUPS, S5_GROUP).transpose(3, 1, 0, 2, 4)
    return hs.reshape(S5_GROUPS, S5_ROWS, S5_W).astype(bf16)


def _s5_unflatten(y):
    ys = y.reshape(S5_GROUPS, S5_NCH, B, S5_L, S5_GROUP).transpose(2, 1, 3, 0, 4).reshape(B, T, D)
    return jnp.concatenate([ys[:, CTX:], ys[:, :CTX]], axis=1)


GLU_TM = 544


def _glu_kernel(m_ref, y_ref, x_ref, w_ref, b_ref, o_ref):
    b = pl.program_id(0)
    t = pl.program_id(1)
    _, _, gate = _row_modulation(m_ref, b, t, GLU_TM)
    z = jax.nn.gelu(y_ref[0]).astype(bf16)
    p = jnp.dot(z, w_ref[...], preferred_element_type=f32) + b_ref[...]
    o_ref[0] = x_ref[0] + gate * (p[:, :D] * jax.nn.sigmoid(p[:, D:]))


def _glu_residual(tok, y, m3, glu_w, glu_b):
    tile = pl.BlockSpec((1, GLU_TM, D), lambda b, t: (b, t, 0))
    return pl.pallas_call(
        _glu_kernel,
        grid=(B, T // GLU_TM),
        in_specs=[
            pl.BlockSpec((MOD_ROWS, 3, D), lambda b, t: (0, 0, 0)),
            tile, tile,
            pl.BlockSpec((D, 2 * D), lambda b, t: (0, 0)),
            pl.BlockSpec((1, 2 * D), lambda b, t: (0, 0)),
        ],
        out_specs=tile,
        out_shape=jax.ShapeDtypeStruct((B, T, D), f32),
        compiler_params=pltpu.CompilerParams(
            dimension_semantics=("arbitrary", "arbitrary"), vmem_limit_bytes=V7X_VMEM_LIMIT),
        name="glu_residual",
    )(m3, y, tok, glu_w, glu_b)


def _cmul(ar, ai, br, bi):
    return ar * br - ai * bi, ar * bi + ai * br


def _s5_combine(e1, e2):
    a1r, a1i, b1r, b1i = e1
    a2r, a2i, b2r, b2i = e2
    ar, ai = _cmul(a2r, a2i, a1r, a1i)
    br, bi = _cmul(a2r, a2i, b1r, b1i)
    return ar, ai, br + b2r, bi + b2i


def _s5_states(u, a_re, a_im, log_dt, b_re, b_im):
    dt = jnp.exp(log_dt)[:, None]
    mag = jnp.exp(dt * a_re)
    ang = dt * a_im
    abar_re, abar_im = mag * jnp.cos(ang), mag * jnp.sin(ang)
    den = a_re * a_re + a_im * a_im
    n_re = abar_re - 1.0
    f_re = (n_re * a_re + abar_im * a_im) / den
    f_im = (abar_im * a_re - n_re * a_im) / den
    bu_re = jnp.einsum('btgc,gpc->btgp', u, b_re)
    bu_im = jnp.einsum('btgc,gpc->btgp', u, b_im)
    x_re, x_im = _cmul(f_re, f_im, bu_re, bu_im)
    a_shape = (1, u.shape[1]) + abar_re.shape
    _, _, h_re, h_im = lax.associative_scan(
        _s5_combine,
        (jnp.broadcast_to(abar_re, a_shape), jnp.broadcast_to(abar_im, a_shape), x_re, x_im),
        axis=1)
    return h_re, h_im


def _s5_mixer_jax(hx, hc, a_re, a_im, log_dt, b_re, b_im, c_re, c_im, d, glu_w, glu_b):
    bsz, seq_len, _ = hx.shape
    ctx_len = hc.shape[1]
    ux = hx.reshape(bsz, seq_len, S5_GROUPS, S5_GROUP)
    uc = hc.reshape(bsz, ctx_len, S5_GROUPS, S5_GROUP)
    yx = d * hx
    yc = d * hc
    for di in range(2):
        if di == 0:
            u = jnp.concatenate([uc, ux], axis=1)
        else:
            u = jnp.concatenate([jnp.flip(uc, 1), jnp.flip(ux, 1)], axis=1)
        h_re, h_im = _s5_states(u, a_re[di], a_im[di], log_dt[di], b_re[di], b_im[di])

        def readout(hr, hi):
            y = (jnp.einsum('btgp,gcp->btgc', hr, c_re[di]) - jnp.einsum('btgp,gcp->btgc', hi, c_im[di]))
            y = y.reshape(bsz, -1, D)
            return jnp.flip(y, 1) if di == 1 else y

        yx = yx + readout(h_re[:, ctx_len:], h_im[:, ctx_len:])
        yc = yc + readout(h_re[:, :ctx_len], h_im[:, :ctx_len])

    def glu(y):
        z = jax.nn.gelu(y)
        p = z @ glu_w + glu_b
        return p[..., :D] * jax.nn.sigmoid(p[..., D:])

    return glu(yx), glu(yc)


def _to_col_major(h, rows):
    b, n, d = h.shape
    return h.reshape(b, rows, GRID_W, d).transpose(0, 2, 1, 3).reshape(b, n, d)


def _to_row_major(h, rows):
    b, n, d = h.shape
    return h.reshape(b, GRID_W, rows, d).transpose(0, 2, 1, 3).reshape(b, n, d)


def _short_conv(u, w):
    pad = GDN_CONV // 2
    return lax.conv_general_dilated(
        u, w[:, None, :].astype(u.dtype), window_strides=(1,), padding=[(pad, pad)],
        dimension_numbers=('NWC', 'WIO', 'NWC'), feature_group_count=u.shape[-1])


def _l2n(t):
    return t * lax.rsqrt(jnp.sum(t * t, axis=-1, keepdims=True) + EPS)


def _gdn_branch(p, conv_w, a_log, dt_bias):
    b, n, _ = p.shape
    qkv = jax.nn.silu(_short_conv(p[..., :3 * D], conv_w))
    q = _l2n(qkv[..., :D].reshape(b, n, GDN_HEADS, GDN_DK)) * (GDN_DK ** -0.5)
    k = _l2n(qkv[..., D:2 * D].reshape(b, n, GDN_HEADS, GDN_DK))
    v = qkv[..., 2 * D:].reshape(b, n, GDN_HEADS, GDN_DV)
    z = p[..., 3 * D:4 * D]
    o0 = 4 * D
    beta = jax.nn.sigmoid(p[..., o0:o0 + 2 * GDN_HEADS]).reshape(b, n, 2, GDN_HEADS)
    alpha = p[..., o0 + 2 * GDN_HEADS:].reshape(b, n, 2, GDN_HEADS)
    g = -jnp.exp(a_log) * jax.nn.softplus(alpha + dt_bias)
    return q, k, v, z, beta, g


def _gated_delta_chunked(q, k, v, g, beta):
    bsz, n_pos, nh, dk = q.shape
    dv = v.shape[-1]
    nc = n_pos // GDN_CHUNK

    def chunks(t):
        return t.reshape(bsz, nc, GDN_CHUNK, nh, -1).transpose(0, 3, 1, 2, 4)

    q, k, v = chunks(q), chunks(k), chunks(v)
    g = g.reshape(bsz, nc, GDN_CHUNK, nh).transpose(0, 3, 1, 2)
    beta = beta.reshape(bsz, nc, GDN_CHUNK, nh).transpose(0, 3, 1, 2)
    gcum = jnp.cumsum(g, axis=-1)
    incl = jnp.tril(jnp.ones((GDN_CHUNK, GDN_CHUNK), dtype=bool))
    strict = jnp.tril(jnp.ones((GDN_CHUNK, GDN_CHUNK), dtype=bool), k=-1)
    decay = jnp.exp(jnp.where(incl, gcum[..., :, None] - gcum[..., None, :], -jnp.inf))
    k_beta = k * beta[..., None]
    a_mat = jnp.where(strict, jnp.einsum('bhncd,bhnsd->bhncs', k_beta, k) * decay, 0.0)
    lhs = a_mat + jnp.eye(GDN_CHUNK, dtype=a_mat.dtype)
    u_val = lax.linalg.triangular_solve(lhs, v * beta[..., None], left_side=True, lower=True)
    w_key = lax.linalg.triangular_solve(lhs, k_beta * jnp.exp(gcum)[..., None], left_side=True, lower=True)
    attn = jnp.where(incl, jnp.einsum('bhncd,bhnsd->bhncs', q, k) * decay, 0.0)
    q_dec = q * jnp.exp(gcum)[..., None]
    k_dec = k * jnp.exp(gcum[..., -1:] - gcum)[..., None]
    g_last = jnp.exp(gcum[..., -1])

    def step(state, inp):
        qd, kd, uv, wk, at, gl = inp
        v_new = uv - jnp.einsum('bhcd,bhde->bhce', wk, state)
        o = jnp.einsum('bhcd,bhde->bhce', qd, state) + jnp.einsum('bhcs,bhse->bhce', at, v_new)
        state = state * gl[..., None, None] + jnp.einsum('bhcd,bhce->bhde', kd, v_new)
        return state, o

    xs = tuple(jnp.moveaxis(t, 2, 0) for t in (q_dec, k_dec, u_val, w_key, attn, g_last))
    s0 = jnp.zeros((bsz, nh, dk, dv), f32)
    _, o = lax.scan(step, s0, xs)
    return o.transpose(1, 0, 3, 2, 4).reshape(bsz, n_pos, nh, dv)


def _gdn_mixer_jax(hx, hc, rows, w_in, conv_w, a_log, dt_bias, norm_g, w_out):
    ctx_len = hc.shape[1]
    qx, kx, vx, zx, bx, gx = _gdn_branch(_to_col_major(hx, rows) @ w_in, conv_w, a_log, dt_bias)
    qc, kc, vc, zc, bc, gc = _gdn_branch(hc @ w_in, conv_w, a_log, dt_bias)
    ox, oc = None, None
    for di in range(2):
        def cat(tc, tx):
            if di == 0:
                return jnp.concatenate([tc, tx], axis=1)
            return jnp.concatenate([jnp.flip(tc, 1), jnp.flip(tx, 1)], axis=1)

        o = _gated_delta_chunked(cat(qc, qx), cat(kc, kx), cat(vc, vx),
                                 cat(gc[:, :, di], gx[:, :, di]), cat(bc[:, :, di], bx[:, :, di]))
        o_c, o_x = o[:, :ctx_len], o[:, ctx_len:]
        if di == 1:
            o_c, o_x = jnp.flip(o_c, 1), jnp.flip(o_x, 1)
        ox = o_x if ox is None else ox + o_x
        oc = o_c if oc is None else oc + o_c

    def project(o, z):
        b, n = o.shape[:2]
        of = o
        y = (of * lax.rsqrt(jnp.mean(of * of, axis=-1, keepdims=True) + EPS) * norm_g
             * jax.nn.silu(z.reshape(b, n, GDN_HEADS, GDN_DV)))
        return y.reshape(b, n, D) @ w_out

    yx = _to_row_major(project(ox, zx), rows)
    yc = project(oc, zc)
    return yx, yc


def kernel(x, c, ctx, c_ctx, mod_w, mod_b, norm_g, ffn_w13, ffn_w2, s5_a_re, s5_a_im, s5_log_dt,
           s5_b_re, s5_b_im, s5_c_re, s5_c_im, s5_d, s5_glu_w, s5_glu_b, gdn_w_in, gdn_conv_w,
           gdn_a_log, gdn_dt_bias, gdn_norm_g, gdn_w_out, final_g):
    tok = jnp.concatenate([x, ctx], axis=1)
    mods = _modulation(c, c_ctx, mod_w, mod_b).reshape(DEPTH, MOD_ROWS, N_MOD, D)
    w13 = ffn_w13.astype(bf16)
    w2 = ffn_w2.astype(bf16)
    rows = SEQ // GRID_W
    for i in range(DEPTH):
        m = mods[i]
        tok = _ffn(tok, m[:, 0:3], norm_g[i, 0][None], w13[i, 0], w2[i, 0])
        h = _mixer_norm(tok, m[:, 3:6], norm_g[i, 1][None])
        j = i // 2
        if i % 2 == 0:
            y = _s5_unflatten(_s5_core(_s5_flatten(h), s5_a_re[j], s5_a_im[j], s5_log_dt[j], s5_b_re[j],
                                       s5_b_im[j], s5_c_re[j], s5_c_im[j], s5_d[j]))
            tok = _glu_residual(tok, y, m[:, 3:6], s5_glu_w[j].astype(bf16), s5_glu_b[j][None])
        else:
            hx, hc = h[:, :SEQ], h[:, SEQ:]
            ox, oc = _gdn_mixer_jax(hx, hc, rows, gdn_w_in[j], gdn_conv_w[j], gdn_a_log[j], gdn_dt_bias[j],
                                    gdn_norm_g[j], gdn_w_out[j])
            gate_x = m[:B, 5][:, None, :]
            gate_c = m[B, 5][None, None, :]
            tok = tok + jnp.concatenate([gate_x * ox, gate_c * oc], axis=1)
        tok = _ffn(tok, m[:, 6:9], norm_g[i, 2][None], w13[i, 1], w2[i, 1])
    return _final_norm(tok, final_g[None])
```

```python
import functools

import jax
import jax.numpy as jnp
from jax import lax
from jax.experimental import pallas as pl
from jax.experimental.pallas import tpu as pltpu

D = 1024
B = 4
SEQ = 4096
CTX = 256
T = SEQ + CTX
DEPTH = 4
GRID_W = 64
N_MOD = 9
D_FF = 2816
S5_GROUP = 16
S5_GROUPS = D // S5_GROUP
S5_STATE = 64
GDN_HEADS = 8
GDN_DK = D // GDN_HEADS
GDN_DV = D // GDN_HEADS
GDN_CONV = 5
GDN_CHUNK = 64
EPS = 1e-6

MOD_ROWS = 8
V7X_VMEM_LIMIT = 56 * 1024 * 1024

f32 = jnp.float32
bf16 = jnp.bfloat16


MOD_TN = 2304


def _mod_kernel(c_ref, w_ref, b_ref, o_ref):
    c = c_ref[...]
    s = c * jax.nn.sigmoid(c)
    o_ref[0] = jnp.dot(s, w_ref[0], preferred_element_type=f32,
                       precision=lax.Precision.HIGHEST) + b_ref[0]


def _modulation(c, c_ctx, mod_w, mod_b):
    cc = jnp.zeros((MOD_ROWS, D), f32).at[:B].set(c).at[B].set(c_ctx)
    n = N_MOD * D
    return pl.pallas_call(
        _mod_kernel,
        grid=(DEPTH, n // MOD_TN),
        in_specs=[
            pl.BlockSpec((MOD_ROWS, D), lambda i, j: (0, 0)),
            pl.BlockSpec((1, D, MOD_TN), lambda i, j: (i, 0, j)),
            pl.BlockSpec((1, 1, MOD_TN), lambda i, j: (i, 0, j)),
        ],
        out_specs=pl.BlockSpec((1, MOD_ROWS, MOD_TN), lambda i, j: (i, 0, j)),
        out_shape=jax.ShapeDtypeStruct((DEPTH, MOD_ROWS, n), f32),
        compiler_params=pltpu.CompilerParams(
            dimension_semantics=("arbitrary", "arbitrary"), vmem_limit_bytes=V7X_VMEM_LIMIT),
        name="modulation",
    )(cc, mod_w, mod_b.reshape(DEPTH, 1, n))


def _row_modulation(m_ref, b, t, tm):
    rows = t * tm + lax.broadcasted_iota(jnp.int32, (tm, 1), 0)
    is_ctx = rows >= SEQ
    mx = m_ref[b]
    mc = m_ref[B]
    return tuple(jnp.where(is_ctx, mc[j:j + 1], mx[j:j + 1]) for j in range(3))


def _ada_norm(x, g, shift, scale):
    ms = jnp.mean(x * x, axis=-1, keepdims=True)
    return x * lax.rsqrt(ms + EPS) * g * (1.0 + scale) + shift


FFN_TM = 544
FFN_CHUNK = 256


def _ffn_kernel(*refs, with_mixer_out):
    if with_mixer_out:
        mm_ref, y_ref, m_ref, g_ref, x_ref, w13_ref, w2_ref, o_ref = refs
    else:
        m_ref, g_ref, x_ref, w13_ref, w2_ref, o_ref = refs
    b = pl.program_id(0)
    t = pl.program_id(1)
    x = x_ref[0]
    if with_mixer_out:
        _, _, gate_mix = _row_modulation(mm_ref, b, t, FFN_TM)
        x = x + gate_mix * y_ref[0]
    shift, scale, gate = _row_modulation(m_ref, b, t, FFN_TM)
    h = _ada_norm(x, g_ref[...], shift, scale).astype(bf16)
    acc = jnp.zeros((FFN_TM, D), f32)
    for j in range(D_FF // FFN_CHUNK):
        lo = j * FFN_CHUNK
        a = jnp.dot(h, w13_ref[:, lo:lo + FFN_CHUNK], preferred_element_type=f32)
        u = jnp.dot(h, w13_ref[:, D_FF + lo:D_FF + lo + FFN_CHUNK], preferred_element_type=f32)
        hid = (a * jax.nn.sigmoid(a) * u).astype(bf16)
        acc = acc + jnp.dot(hid, w2_ref[lo:lo + FFN_CHUNK, :], preferred_element_type=f32)
    o_ref[0] = x + 0.5 * gate * acc


def _ffn(tok, m3, g, w13, w2, mixer_out=None, m3_mixer=None):
    mod = pl.BlockSpec((MOD_ROWS, 3, D), lambda b, t: (0, 0, 0))
    tile = pl.BlockSpec((1, FFN_TM, D), lambda b, t: (b, t, 0))
    in_specs = [
        mod,
        pl.BlockSpec((1, D), lambda b, t: (0, 0)),
        tile,
        pl.BlockSpec((D, 2 * D_FF), lambda b, t: (0, 0), pipeline_mode=pl.Buffered(1)),
        pl.BlockSpec((D_FF, D), lambda b, t: (0, 0), pipeline_mode=pl.Buffered(1)),
    ]
    args = (m3, g, tok, w13, w2)
    if mixer_out is not None:
        in_specs = [mod, tile] + in_specs
        args = (m3_mixer, mixer_out) + args
    return pl.pallas_call(
        functools.partial(_ffn_kernel, with_mixer_out=mixer_out is not None),
        grid=(B, T // FFN_TM),
        in_specs=in_specs,
        out_specs=tile,
        out_shape=jax.ShapeDtypeStruct((B, T, D), f32),
        compiler_params=pltpu.CompilerParams(
            dimension_semantics=("arbitrary", "arbitrary"), vmem_limit_bytes=V7X_VMEM_LIMIT),
        name="ffn",
    )(*args)


NORM_TM = 544


def _norm_kernel(m_ref, g_ref, x_ref, o_ref):
    b = pl.program_id(0)
    t = pl.program_id(1)
    shift, scale, _ = _row_modulation(m_ref, b, t, NORM_TM)
    o_ref[0] = _ada_norm(x_ref[0], g_ref[...], shift, scale).astype(bf16)


def _mixer_norm(tok, m3, g):
    return pl.pallas_call(
        _norm_kernel,
        grid=(B, T // NORM_TM),
        in_specs=[
            pl.BlockSpec((MOD_ROWS, 3, D), lambda b, t: (0, 0, 0)),
            pl.BlockSpec((1, D), lambda b, t: (0, 0)),
            pl.BlockSpec((1, NORM_TM, D), lambda b, t: (b, t, 0)),
        ],
        out_specs=pl.BlockSpec((1, NORM_TM, D), lambda b, t: (b, t, 0)),
        out_shape=jax.ShapeDtypeStruct((B, T, D), bf16),
        compiler_params=pltpu.CompilerParams(dimension_semantics=("arbitrary", "arbitrary")),
        name="mixer_norm",
    )(m3, g, tok)


FINAL_TM = 512


def _final_kernel(g_ref, x_ref, o_ref):
    x = x_ref[0]
    ms = jnp.mean(x * x, axis=-1, keepdims=True)
    o_ref[0] = x * lax.rsqrt(ms + EPS) * g_ref[...]


def _final_norm(tok, g):
    return pl.pallas_call(
        _final_kernel,
        grid=(B, SEQ // FINAL_TM),
        in_specs=[
            pl.BlockSpec((1, D), lambda b, t: (0, 0)),
            pl.BlockSpec((1, FINAL_TM, D), lambda b, t: (b, t, 0)),
        ],
        out_specs=pl.BlockSpec((1, FINAL_TM, D), lambda b, t: (b, t, 0)),
        out_shape=jax.ShapeDtypeStruct((B, SEQ, D), f32),
        compiler_params=pltpu.CompilerParams(dimension_semantics=("arbitrary", "arbitrary")),
        name="final_norm",
    )(g, tok)


S5_L = 64
S5_NCH = T // S5_L
S5_CCH = CTX // S5_L
S5_ROWS = S5_NCH * B
S5_W = S5_L * S5_GROUP
S5_BASE = 8


def _cmul(ar, ai, br, bi):
    return ar * br - ai * bi, ar * bi + ai * br


def _dot_nt(a, b):
    return lax.dot_general(a, b, (((1,), (1,)), ((), ())), preferred_element_type=f32)


def _s5_kernel(u_ref, are_ref, aim_ref, ldt_ref, btre_ref, btim_ref, cre_ref, cim_ref, dt_ref, y_ref,
               m_ref, s_ref, hf_ref, hb_ref):
    P = S5_STATE
    lane = lax.broadcasted_iota(jnp.int32, (1, 2 * P), 1)
    is_f = lane < P
    mf = is_f.astype(f32)
    mb = 1.0 - mf

    a_re = are_ref[0]
    a_im = aim_ref[0]
    dt = jnp.exp(ldt_ref[0])
    mag = jnp.exp(dt * a_re)
    ang = dt * a_im
    l_re = mag * jnp.cos(ang)
    l_im = mag * jnp.sin(ang)
    den = a_re * a_re + a_im * a_im
    n_re = l_re - 1.0
    f_re = (n_re * a_re + l_im * a_im) / den
    f_im = (l_im * a_re - n_re * a_im) / den

    c_re = cre_ref[0]
    c_im = cim_ref[0]
    a_r, a_i = _cmul(f_re, f_im, btre_ref[0], btim_ref[0])
    r_r, r_i = _cmul(l_re, l_im, c_re, c_im)
    p_re, p_im = l_re, l_im
    for _ in range(6):
        fm_re = jnp.where(is_f, p_re, 1.0)
        fm_im = jnp.where(is_f, p_im, 0.0)
        bm_re = jnp.where(is_f, 1.0, p_re)
        bm_im = jnp.where(is_f, 0.0, p_im)
        at_r, at_i = _cmul(fm_re, fm_im, a_r, a_i)
        ab_r, ab_i = _cmul(bm_re, bm_im, a_r, a_i)
        rt_r, rt_i = _cmul(bm_re, bm_im, r_r, r_i)
        rb_r, rb_i = _cmul(fm_re, fm_im, r_r, r_i)
        a_r = jnp.concatenate([at_r, ab_r], axis=0)
        a_i = jnp.concatenate([at_i, ab_i], axis=0)
        r_r = jnp.concatenate([rt_r, rb_r], axis=0)
        r_i = jnp.concatenate([rt_i, rb_i], axis=0)
        p_re, p_im = _cmul(p_re, p_im, p_re, p_im)
    acat = jnp.concatenate([a_r, a_i], axis=1).astype(bf16)
    rcat = jnp.concatenate([r_r, -r_i], axis=1)
    mf2 = jnp.concatenate([mf, mf], axis=1)
    rtf = (rcat * mf2).astype(bf16)
    rtb = (rcat * (1.0 - mf2)).astype(bf16)
    rcat = rcat.astype(bf16)

    nb = S5_BASE
    ccat = jnp.concatenate([c_re, -c_im], axis=1)
    ctf = jnp.concatenate([(ccat * mf2).astype(bf16)] * nb, axis=0)
    ctb = jnp.concatenate([(ccat * (1.0 - mf2)).astype(bf16)] * nb, axis=0)
    bw = nb * S5_GROUP
    kf_all = _dot_nt(acat[S5_W - bw:], ctf)
    kb_all = _dot_nt(acat[:bw], ctb)
    lane_t0 = lax.broadcasted_iota(jnp.int32, (S5_GROUP, bw), 1) // S5_GROUP
    lane_i = lax.broadcasted_iota(jnp.int32, (S5_GROUP, bw), 1)
    row_i = lax.broadcasted_iota(jnp.int32, (S5_GROUP, bw), 0)
    dtile = dt_ref[0]
    blocks = []
    for s0 in range(nb):
        acc = jnp.where(lane_i == s0 * S5_GROUP + row_i, dtile, 0.0)
        for lag in range(nb - s0):
            acc = acc + jnp.where(lane_t0 == s0 + lag, kf_all[(nb - 1 - lag) * S5_GROUP:(nb - lag) * S5_GROUP], 0.0)
        for lag in range(s0 + 1):
            acc = acc + jnp.where(lane_t0 == s0 - lag, kb_all[lag * S5_GROUP:(lag + 1) * S5_GROUP], 0.0)
        blocks.append(acc)
    blk0 = jnp.concatenate(blocks, axis=0).astype(bf16)
    for i in range(S5_L // nb):
        m_ref[i * bw:(i + 1) * bw, i * bw:(i + 1) * bw] = blk0
    n = nb
    while n < S5_L:
        w = n * S5_GROUP
        xn = _dot_nt(acat[S5_W - w:], rtf[:w]).astype(bf16)
        yn = _dot_nt(acat[:w], rtb[S5_W - w:]).astype(bf16)
        for j in range(S5_L // (2 * n)):
            o = 2 * j * w
            m_ref[o:o + w, o + w:o + 2 * w] = xn
            m_ref[o + w:o + 2 * w, o:o + w] = yn
        n *= 2

    u = u_ref[0]
    y = jnp.dot(u, m_ref[...], preferred_element_type=f32)
    s_ref[...] = jnp.dot(u, acat, preferred_element_type=f32)

    h_re = jnp.zeros((B, 2 * P), f32)
    h_im = jnp.zeros((B, 2 * P), f32)
    for j in range(S5_NCH):
        kf = j
        kb = S5_CCH - 1 - j if j < S5_CCH else S5_NCH + S5_CCH - 1 - j
        hf_ref[kf * B:(kf + 1) * B, :] = jnp.concatenate([h_re, h_im], axis=1)
        hb_ref[kb * B:(kb + 1) * B, :] = jnp.concatenate([h_re, h_im], axis=1)
        s_re = jnp.where(is_f, s_ref[kf * B:(kf + 1) * B, 0:2 * P], s_ref[kb * B:(kb + 1) * B, 0:2 * P])
        s_im = jnp.where(is_f, s_ref[kf * B:(kf + 1) * B, 2 * P:4 * P], s_ref[kb * B:(kb + 1) * B, 2 * P:4 * P])
        h_re, h_im = (p_re * h_re - p_im * h_im + s_re, p_re * h_im + p_im * h_re + s_im)
    hcat = jnp.where(mf2 > 0.5, hf_ref[...], hb_ref[...]).astype(bf16)
    y_ref[0] = y + _dot_nt(hcat, rcat)


def _s5_core(uflat, a_re, a_im, log_dt, b_re, b_im, c_re, c_im, d):
    G = S5_GROUPS

    def pack(t):
        return jnp.concatenate([t[0], t[1]], axis=-1)

    are = pack(a_re)[:, None, :]
    aim = pack(a_im)[:, None, :]
    ldt = pack(jnp.broadcast_to(log_dt[:, :, None], (2, G, S5_STATE)))[:, None, :]
    btre = pack(jnp.swapaxes(b_re, -1, -2))
    btim = pack(jnp.swapaxes(b_im, -1, -2))
    cre = pack(c_re)
    cim = pack(c_im)
    dtile = jnp.tile(d.reshape(G, 1, S5_GROUP), (1, 1, S5_BASE))
    vec = pl.BlockSpec((1, 1, 2 * S5_STATE), lambda g: (g, 0, 0))
    mat = pl.BlockSpec((1, S5_GROUP, 2 * S5_STATE), lambda g: (g, 0, 0))
    big = pl.BlockSpec((1, S5_ROWS, S5_W), lambda g: (g, 0, 0))
    return pl.pallas_call(
        _s5_kernel,
        grid=(G,),
        in_specs=[big, vec, vec, vec, mat, mat, mat, mat, vec],
        out_specs=big,
        out_shape=jax.ShapeDtypeStruct((G, S5_ROWS, S5_W), f32),
        scratch_shapes=[
            pltpu.VMEM((S5_W, S5_W), bf16),
            pltpu.VMEM((S5_ROWS, 4 * S5_STATE), f32),
            pltpu.VMEM((S5_ROWS, 4 * S5_STATE), f32),
            pltpu.VMEM((S5_ROWS, 4 * S5_STATE), f32),
        ],
        compiler_params=pltpu.CompilerParams(
            dimension_semantics=("arbitrary",), vmem_limit_bytes=V7X_VMEM_LIMIT),
        name="s5_core",
    )(uflat, are, aim, ldt, btre, btim, cre, cim, dtile)


def _s5_flatten(h):
    hs = jnp.concatenate([h[:, SEQ:], h[:, :SEQ]], axis=1)
    hs = hs.reshape(B, S5_NCH, S5_L, S5_GROUPS, S5_GROUP).transpose(3, 1, 0, 2, 4)
    return hs.reshape(S5_GROUPS, S5_ROWS, S5_W).astype(bf16)


def _s5_unflatten(y):
    ys = y.reshape(S5_GROUPS, S5_NCH, B, S5_L, S5_GROUP).transpose(2, 1, 3, 0, 4).reshape(B, T, D)
    return jnp.concatenate([ys[:, CTX:], ys[:, :CTX]], axis=1)


GLU_TM = 544


def _glu_kernel(m_ref, y_ref, x_ref, w_ref, b_ref, o_ref):
    b = pl.program_id(0)
    t = pl.program_id(1)
    _, _, gate = _row_modulation(m_ref, b, t, GLU_TM)
    z = jax.nn.gelu(y_ref[0]).astype(bf16)
    p = jnp.dot(z, w_ref[...], preferred_element_type=f32) + b_ref[...]
    o_ref[0] = x_ref[0] + gate * (p[:, :D] * jax.nn.sigmoid(p[:, D:]))


def _glu_residual(tok, y, m3, glu_w, glu_b):
    tile = pl.BlockSpec((1, GLU_TM, D), lambda b, t: (b, t, 0))
    return pl.pallas_call(
        _glu_kernel,
        grid=(B, T // GLU_TM),
        in_specs=[
            pl.BlockSpec((MOD_ROWS, 3, D), lambda b, t: (0, 0, 0)),
            tile, tile,
            pl.BlockSpec((D, 2 * D), lambda b, t: (0, 0)),
            pl.BlockSpec((1, 2 * D), lambda b, t: (0, 0)),
        ],
        out_specs=tile,
        out_shape=jax.ShapeDtypeStruct((B, T, D), f32),
        compiler_params=pltpu.CompilerParams(
            dimension_semantics=("arbitrary", "arbitrary"), vmem_limit_bytes=V7X_VMEM_LIMIT),
        name="glu_residual",
    )(m3, y, tok, glu_w, glu_b)


GDN_TM = 256
GDN_NT = T // GDN_TM
GDN_HALO = 16
GDN_QKV = 3 * D
GDN_NPROJ = 4 * D + 128
GDN_NCH = T // GDN_CHUNK
GDN_CCH = CTX // GDN_CHUNK
GDN_HP = 2


def _gdn_proj_kernel(h_ref, hp_ref, hn_ref, w_ref, cw_ref, alog_ref, dtb_ref,
                     q_ref, k_ref, v_ref, z_ref, gb_ref, p_ref):
    j = pl.program_id(1)
    pad = GDN_CONV // 2
    hm = h_ref[0]
    lhs = jnp.concatenate([hp_ref[0], hm, hn_ref[0]], axis=0)
    p_ref[...] = jnp.dot(lhs, w_ref[:, 0:GDN_QKV], preferred_element_type=f32)
    lo = GDN_HALO
    hi = GDN_HALO + GDN_TM

    @pl.when((j == 0) | (j == GDN_NT - 1))
    def _():
        p_ref[lo - 8:lo, :] = jnp.zeros((8, GDN_QKV), f32)

    @pl.when(j >= GDN_NT - 2)
    def _():
        p_ref[hi:hi + 8, :] = jnp.zeros((8, GDN_QKV), f32)

    for cb in range(GDN_QKV // GDN_DK):
        cols = slice(cb * GDN_DK, (cb + 1) * GDN_DK)
        acc = cw_ref[0:1, cols] * p_ref[lo - pad:hi - pad, cols]
        for kk in range(1, GDN_CONV):
            acc = acc + cw_ref[kk:kk + 1, cols] * p_ref[lo - pad + kk:hi - pad + kk, cols]
        xa = acc * jax.nn.sigmoid(acc)
        if cb < 2 * GDN_HEADS:
            xa = xa * lax.rsqrt(jnp.sum(xa * xa, axis=1, keepdims=True) + EPS)
        if cb < GDN_HEADS:
            q_ref[0, :, cols] = (xa * (GDN_DK ** -0.5)).astype(bf16)
        elif cb < 2 * GDN_HEADS:
            k_ref[0, :, slice((cb - GDN_HEADS) * GDN_DK, (cb - GDN_HEADS + 1) * GDN_DK)] = xa.astype(bf16)
        else:
            v_ref[0, :, slice((cb - 2 * GDN_HEADS) * GDN_DV, (cb - 2 * GDN_HEADS + 1) * GDN_DV)] = xa.astype(bf16)

    z_ref[0] = jnp.dot(hm, w_ref[:, GDN_QKV:4 * D], preferred_element_type=f32).astype(bf16)

    ba = jnp.dot(hm, w_ref[:, 4 * D:GDN_NPROJ], preferred_element_type=f32)
    xs = ba + dtb_ref[...]
    softplus = jnp.maximum(xs, 0.0) + jnp.log(1.0 + jnp.exp(-jnp.abs(xs)))
    g = -jnp.exp(alog_ref[...]) * softplus
    ri = lax.broadcasted_iota(jnp.int32, (GDN_TM, GDN_TM), 0)
    ci = lax.broadcasted_iota(jnp.int32, (GDN_TM, GDN_TM), 1)
    same = (ri // GDN_CHUNK) == (ci // GDN_CHUNK)
    tri_f = (same & (ci <= ri)).astype(f32)
    tri_b = (same & (ci >= ri)).astype(f32)
    gc_f = jnp.dot(tri_f, g, preferred_element_type=f32, precision=lax.Precision.HIGHEST)
    gc_b = jnp.dot(tri_b, g, preferred_element_type=f32, precision=lax.Precision.HIGHEST)
    lane = lax.broadcasted_iota(jnp.int32, (GDN_TM, 128), 1)
    gb_ref[0] = jnp.where(lane < GDN_HEADS, gc_f,
                          jnp.where(lane < 2 * GDN_HEADS, gc_b,
                                    jnp.where(lane < 4 * GDN_HEADS, jax.nn.sigmoid(ba), 0.0)))


def _gdn_proj(hseq, w_in, conv_w, a_log, dt_bias):
    nh = 2 * GDN_HEADS
    w = jnp.concatenate([w_in[:, :4 * D], w_in[:, 4 * D + nh:], w_in[:, 4 * D:4 * D + nh],
                         jnp.zeros((D, 128 - 2 * nh), f32)], axis=1).astype(bf16)
    alog = jnp.zeros((1, 128), f32).at[0, :nh].set(a_log.reshape(nh))
    dtb = jnp.zeros((1, 128), f32).at[0, :nh].set(dt_bias.reshape(nh))
    hb = GDN_TM // GDN_HALO
    nhb = T // GDN_HALO
    tile = pl.BlockSpec((1, GDN_TM, D), lambda b, j: (b, j, 0))
    return pl.pallas_call(
        _gdn_proj_kernel,
        grid=(B, GDN_NT),
        in_specs=[
            tile,
            pl.BlockSpec((1, GDN_HALO, D), lambda b, j: (b, jnp.maximum(j * hb - 1, 0), 0)),
            pl.BlockSpec((1, GDN_HALO, D), lambda b, j: (b, jnp.minimum((j + 1) * hb, nhb - 1), 0)),
            pl.BlockSpec((D, GDN_NPROJ), lambda b, j: (0, 0), pipeline_mode=pl.Buffered(1)),
            pl.BlockSpec((GDN_CONV, GDN_QKV), lambda b, j: (0, 0)),
            pl.BlockSpec((1, 128), lambda b, j: (0, 0)),
            pl.BlockSpec((1, 128), lambda b, j: (0, 0)),
        ],
        out_specs=[tile, tile, tile, tile, pl.BlockSpec((1, GDN_TM, 128), lambda b, j: (b, j, 0))],
        out_shape=[jax.ShapeDtypeStruct((B, T, D), bf16)] * 4 + [jax.ShapeDtypeStruct((B, T, 128), f32)],
        scratch_shapes=[pltpu.VMEM((GDN_TM + 2 * GDN_HALO, GDN_QKV), f32)],
        compiler_params=pltpu.CompilerParams(
            dimension_semantics=("arbitrary", "arbitrary"), vmem_limit_bytes=V7X_VMEM_LIMIT),
        name="gdn_proj",
    )(hseq, hseq, hseq, w, conv_w, alog, dtb)


GDN_PA = 4
GDN_INV_BASE = 8


def _gdn_chunk_kernel(q_ref, k_ref, v_ref, gb_ref, o_ref, s_ref, wq_sc, kd_sc, u_sc, at_sc, gl_sc):
    hp = pl.program_id(1)
    C = GDN_CHUNK
    nchain = 2 * GDN_HP
    o_ref[...] = jnp.zeros_like(o_ref)
    s_ref[...] = jnp.zeros_like(s_ref)
    ri = lax.broadcasted_iota(jnp.int32, (C, C), 0)
    ci = lax.broadcasted_iota(jnp.int32, (C, C), 1)
    eye = (ri == ci).astype(f32)
    lane = lax.broadcasted_iota(jnp.int32, (C, 128), 1)
    ones3 = jnp.ones((C, 3 * 128), bf16)
    masks = ((ci <= ri, ci < ri, C - 1), (ci >= ri, ci > ri, 0))
    diag_blk = (ri // GDN_INV_BASE) == (ci // GDN_INV_BASE)
    merge_masks = []
    n = GDN_INV_BASE
    while n < C:
        merge_masks.append(((ri // (2 * n)) == (ci // (2 * n))) & ((ri // n) != (ci // n)))
        n *= 2

    def local(m, carry):
        items = [(hl, d, e) for hl in range(GDN_HP) for d in range(2) for e in range(GDN_PA)]
        st = []
        for hl, d, e in items:
            ch = m * GDN_PA + e
            r0 = pl.multiple_of(ch * C, C)
            cols = slice(hl * GDN_DK, (hl + 1) * GDN_DK)
            q = q_ref[0, pl.ds(r0, C), cols]
            k = k_ref[0, pl.ds(r0, C), cols]
            v = v_ref[0, pl.ds(r0, C), cols].astype(f32)
            gbt = gb_ref[0, pl.ds(r0, C), :]
            lg = d * GDN_HEADS + hp * GDN_HP + hl
            gsel = jnp.where(lane == lg, gbt, 0.0)
            gc = jnp.sum(gsel, axis=1, keepdims=True)
            beta = jnp.sum(jnp.where(lane == 2 * GDN_HEADS + lg, gbt, 0.0), axis=1, keepdims=True)
            a1 = gsel.astype(bf16)
            r1 = gsel - a1.astype(f32)
            a2 = r1.astype(bf16)
            a3 = (r1 - a2.astype(f32)).astype(bf16)
            kf = k.astype(f32)
            kb = kf * beta
            st.append(dict(c=hl * 2 + d, d=d, ch=ch, q=q, k=k, v=v, gc=gc, beta=beta, kf=kf, kb=kb,
                           a3=jnp.concatenate([a1, a2, a3], axis=1)))
        for t in st:
            t["gc_row"] = _dot_nt(ones3, t.pop("a3"))
            t["kq"] = _dot_nt(jnp.concatenate([t["kb"].astype(bf16), t["q"]], axis=0), t["k"])
        for t in st:
            incl, strict, last = masks[t["d"]]
            decay = jnp.exp(jnp.where(incl, t["gc"] - t.pop("gc_row"), -jnp.inf))
            kq = t.pop("kq")
            a_mat = jnp.where(strict, kq[:C] * decay, 0.0)
            at_sc[t["c"], t["ch"]] = jnp.where(incl, kq[C:] * decay, 0.0).astype(bf16)
            t["a"] = a_mat
            x = -jnp.where(diag_blk, a_mat, 0.0)
            t["r"] = eye + x
            t["xb"] = x.astype(bf16)
        for t in st:
            xb = t.pop("xb")
            t["y"] = jnp.dot(xb, xb, preferred_element_type=f32)
        for t in st:
            yb = t.pop("y").astype(bf16)
            ry = jnp.dot(jnp.concatenate([t["r"].astype(bf16), yb], axis=0), yb, preferred_element_type=f32)
            t["r"] = t["r"] + ry[:C]
            t["y"] = ry[C:]
        for t in st:
            t["r"] = t["r"] + jnp.dot(t["r"].astype(bf16), t.pop("y").astype(bf16), preferred_element_type=f32)
        for off in merge_masks:
            for t in st:
                t["w"] = jnp.dot(jnp.where(off, t["a"], 0.0).astype(bf16), t["r"].astype(bf16),
                                 preferred_element_type=f32)
            for t in st:
                t["r"] = t["r"] - jnp.dot(t["r"].astype(bf16), t.pop("w").astype(bf16), preferred_element_type=f32)
        for t in st:
            incl, strict, last = masks[t["d"]]
            gc = t["gc"]
            gexp = jnp.exp(gc)
            rhs = jnp.concatenate([t["v"] * t["beta"], t["kb"] * gexp], axis=1)
            uw = rhs + jnp.dot((t["r"] - eye).astype(bf16), rhs.astype(bf16), preferred_element_type=f32)
            gc_last = gc[last:last + 1]
            c, ch = t["c"], t["ch"]
            u_sc[c, ch] = uw[:, :GDN_DV].astype(bf16)
            wq_sc[c, ch, 0:C, :] = uw[:, GDN_DV:].astype(bf16)
            wq_sc[c, ch, C:2 * C, :] = (t["q"].astype(f32) * gexp).astype(bf16)
            kd_sc[c, ch] = (t["kf"] * jnp.exp(gc_last - gc)).astype(bf16)
            gl_sc[c, ch] = jnp.broadcast_to(jnp.exp(gc_last), (8, 128))
        return carry

    lax.fori_loop(0, GDN_NCH // GDN_PA, local, 0)

    def recur(i, carry):
        st = []
        for hl in range(GDN_HP):
            for d in range(2):
                if d == 0:
                    ch = jnp.where(i < GDN_CCH, SEQ // C + i, i - GDN_CCH)
                else:
                    ch = jnp.where(i < GDN_CCH, SEQ // C + GDN_CCH - 1 - i, GDN_NCH - 1 - i)
                st.append(dict(c=hl * 2 + d, hl=hl, ch=ch))
        for t in st:
            c, ch = t["c"], t["ch"]
            t["s"] = s_ref[c]
            t["wq"] = jnp.dot(wq_sc[c, ch], t["s"].astype(bf16), preferred_element_type=f32)
        for t in st:
            c, ch = t["c"], t["ch"]
            wq = t.pop("wq")
            v_new = (u_sc[c, ch].astype(f32) - wq[:C]).astype(bf16)
            o = wq[C:] + jnp.dot(at_sc[c, ch], v_new, preferred_element_type=f32)
            s_ref[c] = t.pop("s") * gl_sc[c, ch][0:1, :] + lax.dot_general(
                kd_sc[c, ch], v_new, (((0,), (0,)), ((), ())), preferred_element_type=f32)
            r0 = pl.multiple_of(ch * C, C)
            o_ref[0, pl.ds(r0, C), slice(t["hl"] * GDN_DV, (t["hl"] + 1) * GDN_DV)] += o
        return carry

    lax.fori_loop(0, GDN_NCH, recur, 0)


def _gdn_chunk(q, k, v, gb):
    wide = GDN_HP * GDN_DK
    nchain = 2 * GDN_HP
    blk = pl.BlockSpec((1, T, wide), lambda b, hp: (b, 0, hp))
    return pl.pallas_call(
        _gdn_chunk_kernel,
        grid=(B, GDN_HEADS // GDN_HP),
        in_specs=[blk, blk, blk, pl.BlockSpec((1, T, 128), lambda b, hp: (b, 0, 0))],
        out_specs=blk,
        out_shape=jax.ShapeDtypeStruct((B, T, D), f32),
        scratch_shapes=[
            pltpu.VMEM((nchain, GDN_DK, GDN_DV), f32),
            pltpu.VMEM((nchain, GDN_NCH, 2 * GDN_CHUNK, GDN_DK), bf16),
            pltpu.VMEM((nchain, GDN_NCH, GDN_CHUNK, GDN_DK), bf16),
            pltpu.VMEM((nchain, GDN_NCH, GDN_CHUNK, GDN_DV), bf16),
            pltpu.VMEM((nchain, GDN_NCH, GDN_CHUNK, GDN_CHUNK), bf16),
            pltpu.VMEM((nchain, GDN_NCH, 8, 128), f32),
        ],
        compiler_params=pltpu.CompilerParams(
            dimension_semantics=("arbitrary", "arbitrary"), vmem_limit_bytes=V7X_VMEM_LIMIT),
        name="gdn_chunk",
    )(q, k, v, gb)


def _gdn_out_kernel(o_ref, z_ref, ng_ref, w_ref, y_ref):
    parts = []
    for h in range(GDN_HEADS):
        cols = slice(h * GDN_DV, (h + 1) * GDN_DV)
        oh = o_ref[0, :, cols]
        zz = z_ref[0, :, cols].astype(f32)
        yn = oh * lax.rsqrt(jnp.mean(oh * oh, axis=1, keepdims=True) + EPS) * ng_ref[...]
        parts.append((yn * (zz * jax.nn.sigmoid(zz))).astype(bf16))
    y_ref[0] = jnp.dot(jnp.concatenate(parts, axis=1), w_ref[...], preferred_element_type=f32)


def _gdn_out(o, z, norm_g, w_out):
    tile = pl.BlockSpec((1, GDN_TM, D), lambda b, j: (b, j, 0))
    return pl.pallas_call(
        _gdn_out_kernel,
        grid=(B, GDN_NT),
        in_specs=[tile, tile, pl.BlockSpec((1, GDN_DV), lambda b, j: (0, 0)),
                  pl.BlockSpec((D, D), lambda b, j: (0, 0))],
        out_specs=tile,
        out_shape=jax.ShapeDtypeStruct((B, T, D), f32),
        compiler_params=pltpu.CompilerParams(
            dimension_semantics=("arbitrary", "arbitrary"), vmem_limit_bytes=V7X_VMEM_LIMIT),
        name="gdn_out",
    )(o, z, norm_g, w_out)


def _seq_order(h):
    rows = SEQ // GRID_W
    lat = h[:, :SEQ].reshape(B, rows, GRID_W, D).transpose(0, 2, 1, 3).reshape(B, SEQ, D)
    return jnp.concatenate([lat, h[:, SEQ:]], axis=1)


def _token_order(y):
    rows = SEQ // GRID_W
    lat = y[:, :SEQ].reshape(B, GRID_W, rows, D).transpose(0, 2, 1, 3).reshape(B, SEQ, D)
    return jnp.concatenate([lat, y[:, SEQ:]], axis=1)


def _gdn_mixer(h, w_in, conv_w, a_log, dt_bias, norm_g, w_out):
    q, k, v, z, gb = _gdn_proj(_seq_order(h), w_in, conv_w, a_log, dt_bias)
    o = _gdn_chunk(q, k, v, gb)
    return _token_order(_gdn_out(o, z, norm_g[None], w_out.astype(bf16)))


def kernel(x, c, ctx, c_ctx, mod_w, mod_b, norm_g, ffn_w13, ffn_w2, s5_a_re, s5_a_im, s5_log_dt,
           s5_b_re, s5_b_im, s5_c_re, s5_c_im, s5_d, s5_glu_w, s5_glu_b, gdn_w_in, gdn_conv_w,
           gdn_a_log, gdn_dt_bias, gdn_norm_g, gdn_w_out, final_g):
    tok = jnp.concatenate([x, ctx], axis=1)
    mods = _modulation(c, c_ctx, mod_w, mod_b).reshape(DEPTH, MOD_ROWS, N_MOD, D)
    w13 = ffn_w13.astype(bf16)
    w2 = ffn_w2.astype(bf16)
    for i in range(DEPTH):
        m = mods[i]
        tok = _ffn(tok, m[:, 0:3], norm_g[i, 0][None], w13[i, 0], w2[i, 0])
        h = _mixer_norm(tok, m[:, 3:6], norm_g[i, 1][None])
        j = i // 2
        if i % 2 == 0:
            y = _s5_unflatten(_s5_core(_s5_flatten(h), s5_a_re[j], s5_a_im[j], s5_log_dt[j], s5_b_re[j],
                                       s5_b_im[j], s5_c_re[j], s5_c_im[j], s5_d[j]))
            tok = _glu_residual(tok, y, m[:, 3:6], s5_glu_w[j].astype(bf16), s5_glu_b[j][None])
            tok = _ffn(tok, m[:, 6:9], norm_g[i, 2][None], w13[i, 1], w2[i, 1])
        else:
            y = _gdn_mixer(h, gdn_w_in[j], gdn_conv_w[j], gdn_a_log[j], gdn_dt_bias[j], gdn_norm_g[j],
                           gdn_w_out[j])
            tok = _ffn(tok, m[:, 6:9], norm_g[i, 2][None], w13[i, 1], w2[i, 1], mixer_out=y, m3_mixer=m[:, 3:6])
    return _final_norm(tok, final_g[None])
```

```python
import functools

import jax
import jax.numpy as jnp
from jax import lax
from jax.experimental import pallas as pl
from jax.experimental.pallas import tpu as pltpu

D = 1024
B = 4
SEQ = 4096
CTX = 256
T = SEQ + CTX
DEPTH = 4
GRID_W = 64
N_MOD = 9
D_FF = 2816
S5_GROUP = 16
S5_GROUPS = D // S5_GROUP
S5_STATE = 64
GDN_HEADS = 8
GDN_DK = D // GDN_HEADS
GDN_DV = D // GDN_HEADS
GDN_CONV = 5
GDN_CHUNK = 128
EPS = 1e-6

MOD_ROWS = 8
V7X_VMEM_LIMIT = 56 * 1024 * 1024

f32 = jnp.float32
bf16 = jnp.bfloat16


MOD_TN = 2304


def _mod_kernel(c_ref, w_ref, b_ref, o_ref):
    c = c_ref[...]
    s = c * jax.nn.sigmoid(c)
    o_ref[0] = jnp.dot(s, w_ref[0], preferred_element_type=f32,
                       precision=lax.Precision.HIGHEST) + b_ref[0]


def _modulation(c, c_ctx, mod_w, mod_b):
    cc = jnp.zeros((MOD_ROWS, D), f32).at[:B].set(c).at[B].set(c_ctx)
    n = N_MOD * D
    return pl.pallas_call(
        _mod_kernel,
        grid=(DEPTH, n // MOD_TN),
        in_specs=[
            pl.BlockSpec((MOD_ROWS, D), lambda i, j: (0, 0)),
            pl.BlockSpec((1, D, MOD_TN), lambda i, j: (i, 0, j)),
            pl.BlockSpec((1, 1, MOD_TN), lambda i, j: (i, 0, j)),
        ],
        out_specs=pl.BlockSpec((1, MOD_ROWS, MOD_TN), lambda i, j: (i, 0, j)),
        out_shape=jax.ShapeDtypeStruct((DEPTH, MOD_ROWS, n), f32),
        compiler_params=pltpu.CompilerParams(
            dimension_semantics=("arbitrary", "arbitrary"), vmem_limit_bytes=V7X_VMEM_LIMIT),
        name="modulation",
    )(cc, mod_w, mod_b.reshape(DEPTH, 1, n))


def _row_modulation(m_ref, b, t, tm):
    rows = t * tm + lax.broadcasted_iota(jnp.int32, (tm, 1), 0)
    is_ctx = rows >= SEQ
    mx = m_ref[b]
    mc = m_ref[B]
    return tuple(jnp.where(is_ctx, mc[j:j + 1], mx[j:j + 1]) for j in range(3))


def _ada_norm(x, g, shift, scale):
    ms = jnp.mean(x * x, axis=-1, keepdims=True)
    return x * lax.rsqrt(ms + EPS) * g * (1.0 + scale) + shift


FFN_TM = 1088
FFN_CHUNK = 256


def _ffn_kernel(*refs, with_mixer_out, with_next_norm):
    refs = list(refs)
    if with_mixer_out:
        mm_ref, y_ref = refs[:2]
        refs = refs[2:]
    m_ref, g_ref, x_ref, w13_ref, w2_ref = refs[:5]
    refs = refs[5:]
    if with_next_norm:
        mn_ref, gn_ref, o_ref, h_ref = refs
    else:
        (o_ref,) = refs
    b = pl.program_id(0)
    t = pl.program_id(1)
    x = x_ref[0]
    if with_mixer_out:
        _, _, gate_mix = _row_modulation(mm_ref, b, t, FFN_TM)
        x = x + gate_mix * y_ref[0]
    shift, scale, gate = _row_modulation(m_ref, b, t, FFN_TM)
    h = _ada_norm(x, g_ref[...], shift, scale).astype(bf16)
    acc = jnp.zeros((FFN_TM, D), f32)
    for j in range(D_FF // FFN_CHUNK):
        lo = j * FFN_CHUNK
        a = jnp.dot(h, w13_ref[:, lo:lo + FFN_CHUNK], preferred_element_type=f32)
        u = jnp.dot(h, w13_ref[:, D_FF + lo:D_FF + lo + FFN_CHUNK], preferred_element_type=f32)
        hid = (a * jax.nn.sigmoid(a) * u).astype(bf16)
        acc = acc + jnp.dot(hid, w2_ref[lo:lo + FFN_CHUNK, :], preferred_element_type=f32)
    out = x + 0.5 * gate * acc
    o_ref[0] = out
    if with_next_norm:
        shift_n, scale_n, _ = _row_modulation(mn_ref, b, t, FFN_TM)
        h_ref[0] = _ada_norm(out, gn_ref[...], shift_n, scale_n).astype(bf16)


def _ffn(tok, m3, g, w13, w2, mixer_out=None, m3_mixer=None, next_norm=None):
    mod = pl.BlockSpec((MOD_ROWS, 3, D), lambda b, t: (0, 0, 0))
    gain = pl.BlockSpec((1, D), lambda b, t: (0, 0))
    tile = pl.BlockSpec((1, FFN_TM, D), lambda b, t: (b, t, 0))
    in_specs = [
        mod, gain, tile,
        pl.BlockSpec((D, 2 * D_FF), lambda b, t: (0, 0), pipeline_mode=pl.Buffered(1)),
        pl.BlockSpec((D_FF, D), lambda b, t: (0, 0), pipeline_mode=pl.Buffered(1)),
    ]
    args = (m3, g, tok, w13, w2)
    out_specs = tile
    out_shape = jax.ShapeDtypeStruct((B, T, D), f32)
    if mixer_out is not None:
        in_specs = [mod, tile] + in_specs
        args = (m3_mixer, mixer_out) + args
    if next_norm is not None:
        in_specs = in_specs + [mod, gain]
        args = args + tuple(next_norm)
        out_specs = [tile, tile]
        out_shape = [out_shape, jax.ShapeDtypeStruct((B, T, D), bf16)]
    return pl.pallas_call(
        functools.partial(_ffn_kernel, with_mixer_out=mixer_out is not None,
                          with_next_norm=next_norm is not None),
        grid=(B, T // FFN_TM),
        in_specs=in_specs,
        out_specs=out_specs,
        out_shape=out_shape,
        compiler_params=pltpu.CompilerParams(
            dimension_semantics=("arbitrary", "arbitrary"), vmem_limit_bytes=V7X_VMEM_LIMIT),
        name="ffn",
    )(*args)


FINAL_TM = 512


def _final_kernel(g_ref, x_ref, o_ref):
    x = x_ref[0]
    ms = jnp.mean(x * x, axis=-1, keepdims=True)
    o_ref[0] = x * lax.rsqrt(ms + EPS) * g_ref[...]


def _final_norm(tok, g):
    return pl.pallas_call(
        _final_kernel,
        grid=(B, SEQ // FINAL_TM),
        in_specs=[
            pl.BlockSpec((1, D), lambda b, t: (0, 0)),
            pl.BlockSpec((1, FINAL_TM, D), lambda b, t: (b, t, 0)),
        ],
        out_specs=pl.BlockSpec((1, FINAL_TM, D), lambda b, t: (b, t, 0)),
        out_shape=jax.ShapeDtypeStruct((B, SEQ, D), f32),
        compiler_params=pltpu.CompilerParams(dimension_semantics=("arbitrary", "arbitrary")),
        name="final_norm",
    )(g, tok)


S5_L = 64
S5_NCH = T // S5_L
S5_CCH = CTX // S5_L
S5_ROWS = S5_NCH * B
S5_W = S5_L * S5_GROUP
S5_BASE = 8


def _cmul(ar, ai, br, bi):
    return ar * br - ai * bi, ar * bi + ai * br


def _dot_nt(a, b):
    return lax.dot_general(a, b, (((1,), (1,)), ((), ())), preferred_element_type=f32)


def _s5_kernel(u_ref, are_ref, aim_ref, ldt_ref, btre_ref, btim_ref, cre_ref, cim_ref, dt_ref, y_ref,
               m_ref, s_ref, hf_ref, hb_ref):
    P = S5_STATE
    lane = lax.broadcasted_iota(jnp.int32, (1, 2 * P), 1)
    is_f = lane < P
    mf = is_f.astype(f32)
    mb = 1.0 - mf

    a_re = are_ref[0]
    a_im = aim_ref[0]
    dt = jnp.exp(ldt_ref[0])
    mag = jnp.exp(dt * a_re)
    ang = dt * a_im
    l_re = mag * jnp.cos(ang)
    l_im = mag * jnp.sin(ang)
    den = a_re * a_re + a_im * a_im
    n_re = l_re - 1.0
    f_re = (n_re * a_re + l_im * a_im) / den
    f_im = (l_im * a_re - n_re * a_im) / den

    c_re = cre_ref[0]
    c_im = cim_ref[0]
    a_r, a_i = _cmul(f_re, f_im, btre_ref[0], btim_ref[0])
    r_r, r_i = _cmul(l_re, l_im, c_re, c_im)
    p_re, p_im = l_re, l_im
    for _ in range(6):
        fm_re = jnp.where(is_f, p_re, 1.0)
        fm_im = jnp.where(is_f, p_im, 0.0)
        bm_re = jnp.where(is_f, 1.0, p_re)
        bm_im = jnp.where(is_f, 0.0, p_im)
        at_r, at_i = _cmul(fm_re, fm_im, a_r, a_i)
        ab_r, ab_i = _cmul(bm_re, bm_im, a_r, a_i)
        rt_r, rt_i = _cmul(bm_re, bm_im, r_r, r_i)
        rb_r, rb_i = _cmul(fm_re, fm_im, r_r, r_i)
        a_r = jnp.concatenate([at_r, ab_r], axis=0)
        a_i = jnp.concatenate([at_i, ab_i], axis=0)
        r_r = jnp.concatenate([rt_r, rb_r], axis=0)
        r_i = jnp.concatenate([rt_i, rb_i], axis=0)
        p_re, p_im = _cmul(p_re, p_im, p_re, p_im)
    acat = jnp.concatenate([a_r, a_i], axis=1).astype(bf16)
    rcat = jnp.concatenate([r_r, -r_i], axis=1)
    mf2 = jnp.concatenate([mf, mf], axis=1)
    rtf = (rcat * mf2).astype(bf16)
    rtb = (rcat * (1.0 - mf2)).astype(bf16)
    rcat = rcat.astype(bf16)

    nb = S5_BASE
    ccat = jnp.concatenate([c_re, -c_im], axis=1)
    ctf = jnp.concatenate([(ccat * mf2).astype(bf16)] * nb, axis=0)
    ctb = jnp.concatenate([(ccat * (1.0 - mf2)).astype(bf16)] * nb, axis=0)
    bw = nb * S5_GROUP
    kf_all = _dot_nt(acat[S5_W - bw:], ctf)
    kb_all = _dot_nt(acat[:bw], ctb)
    lane_t0 = lax.broadcasted_iota(jnp.int32, (S5_GROUP, bw), 1) // S5_GROUP
    lane_i = lax.broadcasted_iota(jnp.int32, (S5_GROUP, bw), 1)
    row_i = lax.broadcasted_iota(jnp.int32, (S5_GROUP, bw), 0)
    dtile = dt_ref[0]
    blocks = []
    for s0 in range(nb):
        acc = jnp.where(lane_i == s0 * S5_GROUP + row_i, dtile, 0.0)
        for lag in range(nb - s0):
            acc = acc + jnp.where(lane_t0 == s0 + lag, kf_all[(nb - 1 - lag) * S5_GROUP:(nb - lag) * S5_GROUP], 0.0)
        for lag in range(s0 + 1):
            acc = acc + jnp.where(lane_t0 == s0 - lag, kb_all[lag * S5_GROUP:(lag + 1) * S5_GROUP], 0.0)
        blocks.append(acc)
    blk0 = jnp.concatenate(blocks, axis=0).astype(bf16)
    for i in range(S5_L // nb):
        m_ref[i * bw:(i + 1) * bw, i * bw:(i + 1) * bw] = blk0
    n = nb
    while n < S5_L:
        w = n * S5_GROUP
        xn = _dot_nt(acat[S5_W - w:], rtf[:w]).astype(bf16)
        yn = _dot_nt(acat[:w], rtb[S5_W - w:]).astype(bf16)
        for j in range(S5_L // (2 * n)):
            o = 2 * j * w
            m_ref[o:o + w, o + w:o + 2 * w] = xn
            m_ref[o + w:o + 2 * w, o:o + w] = yn
        n *= 2

    u = u_ref[0]
    y = jnp.dot(u, m_ref[...], preferred_element_type=f32)
    s_ref[...] = jnp.dot(u, acat, preferred_element_type=f32)

    h_re = jnp.zeros((B, 2 * P), f32)
    h_im = jnp.zeros((B, 2 * P), f32)
    for j in range(S5_NCH):
        lat = S5_NCH - S5_CCH
        kf = lat + j if j < S5_CCH else j - S5_CCH
        kb = lat + S5_CCH - 1 - j if j < S5_CCH else S5_NCH - 1 - j
        hf_ref[kf * B:(kf + 1) * B, :] = jnp.concatenate([h_re, h_im], axis=1)
        hb_ref[kb * B:(kb + 1) * B, :] = jnp.concatenate([h_re, h_im], axis=1)
        s_re = jnp.where(is_f, s_ref[kf * B:(kf + 1) * B, 0:2 * P], s_ref[kb * B:(kb + 1) * B, 0:2 * P])
        s_im = jnp.where(is_f, s_ref[kf * B:(kf + 1) * B, 2 * P:4 * P], s_ref[kb * B:(kb + 1) * B, 2 * P:4 * P])
        h_re, h_im = (p_re * h_re - p_im * h_im + s_re, p_re * h_im + p_im * h_re + s_im)
    hcat = jnp.where(mf2 > 0.5, hf_ref[...], hb_ref[...]).astype(bf16)
    y_ref[0] = y + _dot_nt(hcat, rcat)


def _s5_core(uflat, a_re, a_im, log_dt, b_re, b_im, c_re, c_im, d):
    G = S5_GROUPS

    def pack(t):
        return jnp.concatenate([t[0], t[1]], axis=-1)

    are = pack(a_re)[:, None, :]
    aim = pack(a_im)[:, None, :]
    ldt = pack(jnp.broadcast_to(log_dt[:, :, None], (2, G, S5_STATE)))[:, None, :]
    btre = pack(jnp.swapaxes(b_re, -1, -2))
    btim = pack(jnp.swapaxes(b_im, -1, -2))
    cre = pack(c_re)
    cim = pack(c_im)
    dtile = jnp.tile(d.reshape(G, 1, S5_GROUP), (1, 1, S5_BASE))
    vec = pl.BlockSpec((1, 1, 2 * S5_STATE), lambda g: (g, 0, 0))
    mat = pl.BlockSpec((1, S5_GROUP, 2 * S5_STATE), lambda g: (g, 0, 0))
    big = pl.BlockSpec((1, S5_ROWS, S5_W), lambda g: (g, 0, 0))
    return pl.pallas_call(
        _s5_kernel,
        grid=(G,),
        in_specs=[big, vec, vec, vec, mat, mat, mat, mat, vec],
        out_specs=big,
        out_shape=jax.ShapeDtypeStruct((G, S5_ROWS, S5_W), f32),
        scratch_shapes=[
            pltpu.VMEM((S5_W, S5_W), bf16),
            pltpu.VMEM((S5_ROWS, 4 * S5_STATE), f32),
            pltpu.VMEM((S5_ROWS, 4 * S5_STATE), f32),
            pltpu.VMEM((S5_ROWS, 4 * S5_STATE), f32),
        ],
        compiler_params=pltpu.CompilerParams(
            dimension_semantics=("arbitrary",), vmem_limit_bytes=V7X_VMEM_LIMIT),
        name="s5_core",
    )(uflat, are, aim, ldt, btre, btim, cre, cim, dtile)


def _s5_flatten(h):
    hs = h.reshape(B, S5_NCH, S5_L, S5_GROUPS, S5_GROUP).transpose(3, 1, 0, 2, 4)
    return hs.reshape(S5_GROUPS, S5_ROWS, S5_W).astype(bf16)


def _s5_unflatten(y):
    return y.reshape(S5_GROUPS, S5_NCH, B, S5_L, S5_GROUP).transpose(2, 1, 3, 0, 4).reshape(B, T, D)


GLU_TM = 544


def _glu_kernel(m_ref, y_ref, x_ref, w_ref, b_ref, o_ref):
    b = pl.program_id(0)
    t = pl.program_id(1)
    _, _, gate = _row_modulation(m_ref, b, t, GLU_TM)
    z = jax.nn.gelu(y_ref[0]).astype(bf16)
    p = jnp.dot(z, w_ref[...], preferred_element_type=f32) + b_ref[...]
    o_ref[0] = x_ref[0] + gate * (p[:, :D] * jax.nn.sigmoid(p[:, D:]))


def _glu_residual(tok, y, m3, glu_w, glu_b):
    tile = pl.BlockSpec((1, GLU_TM, D), lambda b, t: (b, t, 0))
    return pl.pallas_call(
        _glu_kernel,
        grid=(B, T // GLU_TM),
        in_specs=[
            pl.BlockSpec((MOD_ROWS, 3, D), lambda b, t: (0, 0, 0)),
            tile, tile,
            pl.BlockSpec((D, 2 * D), lambda b, t: (0, 0)),
            pl.BlockSpec((1, 2 * D), lambda b, t: (0, 0)),
        ],
        out_specs=tile,
        out_shape=jax.ShapeDtypeStruct((B, T, D), f32),
        compiler_params=pltpu.CompilerParams(
            dimension_semantics=("arbitrary", "arbitrary"), vmem_limit_bytes=V7X_VMEM_LIMIT),
        name="glu_residual",
    )(m3, y, tok, glu_w, glu_b)


GDN_TM = 256
GDN_NT = T // GDN_TM
GDN_HALO = 16
GDN_QKV = 3 * D
GDN_NPROJ = 4 * D + 128
GDN_NCH = T // GDN_CHUNK
GDN_CCH = CTX // GDN_CHUNK
GDN_HP = 2


def _gdn_proj_kernel(h_ref, hp_ref, hn_ref, w_ref, cw_ref, alog_ref, dtb_ref,
                     q_ref, k_ref, v_ref, z_ref, gb_ref, p_ref):
    j = pl.program_id(1)
    pad = GDN_CONV // 2
    hm = h_ref[0]
    lhs = jnp.concatenate([hp_ref[0], hm, hn_ref[0]], axis=0)
    p_ref[...] = jnp.dot(lhs, w_ref[:, 0:GDN_QKV], preferred_element_type=f32)
    lo = GDN_HALO
    hi = GDN_HALO + GDN_TM

    @pl.when((j == 0) | (j == GDN_NT - 1))
    def _():
        p_ref[lo - 8:lo, :] = jnp.zeros((8, GDN_QKV), f32)

    @pl.when(j >= GDN_NT - 2)
    def _():
        p_ref[hi:hi + 8, :] = jnp.zeros((8, GDN_QKV), f32)

    for cb in range(GDN_QKV // GDN_DK):
        cols = slice(cb * GDN_DK, (cb + 1) * GDN_DK)
        acc = cw_ref[0:1, cols] * p_ref[lo - pad:hi - pad, cols]
        for kk in range(1, GDN_CONV):
            acc = acc + cw_ref[kk:kk + 1, cols] * p_ref[lo - pad + kk:hi - pad + kk, cols]
        xa = acc * jax.nn.sigmoid(acc)
        if cb < 2 * GDN_HEADS:
            xa = xa * lax.rsqrt(jnp.sum(xa * xa, axis=1, keepdims=True) + EPS)
        if cb < GDN_HEADS:
            q_ref[0, :, cols] = (xa * (GDN_DK ** -0.5)).astype(bf16)
        elif cb < 2 * GDN_HEADS:
            k_ref[0, :, slice((cb - GDN_HEADS) * GDN_DK, (cb - GDN_HEADS + 1) * GDN_DK)] = xa.astype(bf16)
        else:
            v_ref[0, :, slice((cb - 2 * GDN_HEADS) * GDN_DV, (cb - 2 * GDN_HEADS + 1) * GDN_DV)] = xa.astype(bf16)

    z_ref[0] = jnp.dot(hm, w_ref[:, GDN_QKV:4 * D], preferred_element_type=f32).astype(bf16)

    ba = jnp.dot(hm, w_ref[:, 4 * D:GDN_NPROJ], preferred_element_type=f32)
    xs = ba + dtb_ref[...]
    softplus = jnp.maximum(xs, 0.0) + jnp.log(1.0 + jnp.exp(-jnp.abs(xs)))
    g = -jnp.exp(alog_ref[...]) * softplus
    ri = lax.broadcasted_iota(jnp.int32, (GDN_TM, GDN_TM), 0)
    ci = lax.broadcasted_iota(jnp.int32, (GDN_TM, GDN_TM), 1)
    same = (ri // GDN_CHUNK) == (ci // GDN_CHUNK)
    tri_f = (same & (ci <= ri)).astype(f32)
    tri_b = (same & (ci >= ri)).astype(f32)
    gc_f = jnp.dot(tri_f, g, preferred_element_type=f32, precision=lax.Precision.HIGHEST)
    gc_b = jnp.dot(tri_b, g, preferred_element_type=f32, precision=lax.Precision.HIGHEST)
    lane = lax.broadcasted_iota(jnp.int32, (GDN_TM, 128), 1)
    gb_ref[0] = jnp.where(lane < GDN_HEADS, gc_f,
                          jnp.where(lane < 2 * GDN_HEADS, gc_b,
                                    jnp.where(lane < 4 * GDN_HEADS, jax.nn.sigmoid(ba), 0.0)))


def _gdn_proj(hseq, w_in, conv_w, a_log, dt_bias):
    nh = 2 * GDN_HEADS
    w = jnp.concatenate([w_in[:, :4 * D], w_in[:, 4 * D + nh:], w_in[:, 4 * D:4 * D + nh],
                         jnp.zeros((D, 128 - 2 * nh), f32)], axis=1).astype(bf16)
    alog = jnp.zeros((1, 128), f32).at[0, :nh].set(a_log.reshape(nh))
    dtb = jnp.zeros((1, 128), f32).at[0, :nh].set(dt_bias.reshape(nh))
    hb = GDN_TM // GDN_HALO
    nhb = T // GDN_HALO
    tile = pl.BlockSpec((1, GDN_TM, D), lambda b, j: (b, j, 0))
    return pl.pallas_call(
        _gdn_proj_kernel,
        grid=(B, GDN_NT),
        in_specs=[
            tile,
            pl.BlockSpec((1, GDN_HALO, D), lambda b, j: (b, jnp.maximum(j * hb - 1, 0), 0)),
            pl.BlockSpec((1, GDN_HALO, D), lambda b, j: (b, jnp.minimum((j + 1) * hb, nhb - 1), 0)),
            pl.BlockSpec((D, GDN_NPROJ), lambda b, j: (0, 0), pipeline_mode=pl.Buffered(1)),
            pl.BlockSpec((GDN_CONV, GDN_QKV), lambda b, j: (0, 0)),
            pl.BlockSpec((1, 128), lambda b, j: (0, 0)),
            pl.BlockSpec((1, 128), lambda b, j: (0, 0)),
        ],
        out_specs=[tile, tile, tile, tile, pl.BlockSpec((1, GDN_TM, 128), lambda b, j: (b, j, 0))],
        out_shape=[jax.ShapeDtypeStruct((B, T, D), bf16)] * 4 + [jax.ShapeDtypeStruct((B, T, 128), f32)],
        scratch_shapes=[pltpu.VMEM((GDN_TM + 2 * GDN_HALO, GDN_QKV), f32)],
        compiler_params=pltpu.CompilerParams(
            dimension_semantics=("arbitrary", "arbitrary"), vmem_limit_bytes=V7X_VMEM_LIMIT),
        name="gdn_proj",
    )(hseq, hseq, hseq, w, conv_w, alog, dtb)


GDN_PA = 2
GDN_INV_BASE = 8


def _gdn_chunk_kernel(q_ref, k_ref, v_ref, gb_ref, o_ref, s_ref, wq_sc, kd_sc, u_sc, at_sc, gl_sc):
    hp = pl.program_id(1)
    C = GDN_CHUNK
    nchain = 2 * GDN_HP
    o_ref[...] = jnp.zeros_like(o_ref)
    s_ref[...] = jnp.zeros_like(s_ref)
    ri = lax.broadcasted_iota(jnp.int32, (C, C), 0)
    ci = lax.broadcasted_iota(jnp.int32, (C, C), 1)
    eye = (ri == ci).astype(f32)
    lane = lax.broadcasted_iota(jnp.int32, (C, 128), 1)
    ones3 = jnp.ones((C, 3 * 128), bf16)
    masks = ((ci <= ri, ci < ri, C - 1), (ci >= ri, ci > ri, 0))
    diag_blk = (ri // GDN_INV_BASE) == (ci // GDN_INV_BASE)
    merge_masks = []
    n = GDN_INV_BASE
    while n < C:
        merge_masks.append(((ri // (2 * n)) == (ci // (2 * n))) & ((ri // n) != (ci // n)))
        n *= 2

    def local(m, carry):
        items = [(hl, d, e) for hl in range(GDN_HP) for d in range(2) for e in range(GDN_PA)]
        st = []
        for hl, d, e in items:
            ch = m * GDN_PA + e
            r0 = pl.multiple_of(ch * C, C)
            cols = slice(hl * GDN_DK, (hl + 1) * GDN_DK)
            q = q_ref[0, pl.ds(r0, C), cols]
            k = k_ref[0, pl.ds(r0, C), cols]
            v = v_ref[0, pl.ds(r0, C), cols].astype(f32)
            gbt = gb_ref[0, pl.ds(r0, C), :]
            lg = d * GDN_HEADS + hp * GDN_HP + hl
            gsel = jnp.where(lane == lg, gbt, 0.0)
            gc = jnp.sum(gsel, axis=1, keepdims=True)
            beta = jnp.sum(jnp.where(lane == 2 * GDN_HEADS + lg, gbt, 0.0), axis=1, keepdims=True)
            a1 = gsel.astype(bf16)
            r1 = gsel - a1.astype(f32)
            a2 = r1.astype(bf16)
            a3 = (r1 - a2.astype(f32)).astype(bf16)
            kf = k.astype(f32)
            kb = kf * beta
            st.append(dict(c=hl * 2 + d, d=d, ch=ch, q=q, k=k, v=v, gc=gc, beta=beta, kf=kf, kb=kb,
                           a3=jnp.concatenate([a1, a2, a3], axis=1)))
        for t in st:
            t["gc_row"] = _dot_nt(ones3, t.pop("a3"))
            t["kq"] = _dot_nt(jnp.concatenate([t["kb"].astype(bf16), t["q"]], axis=0), t["k"])
        for t in st:
            incl, strict, last = masks[t["d"]]
            decay = jnp.exp(jnp.where(incl, t["gc"] - t.pop("gc_row"), -jnp.inf))
            kq = t.pop("kq")
            a_mat = jnp.where(strict, kq[:C] * decay, 0.0)
            at_sc[t["c"], t["ch"]] = jnp.where(incl, kq[C:] * decay, 0.0).astype(bf16)
            t["a"] = a_mat
            x = -jnp.where(diag_blk, a_mat, 0.0)
            t["r"] = eye + x
            t["xb"] = x.astype(bf16)
        for t in st:
            xb = t.pop("xb")
            t["y"] = jnp.dot(xb, xb, preferred_element_type=f32)
        for t in st:
            yb = t.pop("y").astype(bf16)
            ry = jnp.dot(jnp.concatenate([t["r"].astype(bf16), yb], axis=0), yb, preferred_element_type=f32)
            t["r"] = t["r"] + ry[:C]
            t["y"] = ry[C:]
        for t in st:
            t["r"] = t["r"] + jnp.dot(t["r"].astype(bf16), t.pop("y").astype(bf16), preferred_element_type=f32)
        for off in merge_masks:
            for t in st:
                t["w"] = jnp.dot(jnp.where(off, t["a"], 0.0).astype(bf16), t["r"].astype(bf16),
                                 preferred_element_type=f32)
            for t in st:
                t["r"] = t["r"] - jnp.dot(t["r"].astype(bf16), t.pop("w").astype(bf16), preferred_element_type=f32)
        for t in st:
            incl, strict, last = masks[t["d"]]
            gc = t["gc"]
            gexp = jnp.exp(gc)
            rhs = jnp.concatenate([t["v"] * t["beta"], t["kb"] * gexp], axis=1)
            uw = rhs + jnp.dot((t["r"] - eye).astype(bf16), rhs.astype(bf16), preferred_element_type=f32)
            gc_last = gc[last:last + 1]
            c, ch = t["c"], t["ch"]
            u_sc[c, ch] = uw[:, :GDN_DV].astype(bf16)
            wq_sc[c, ch, 0:C, :] = uw[:, GDN_DV:].astype(bf16)
            wq_sc[c, ch, C:2 * C, :] = (t["q"].astype(f32) * gexp).astype(bf16)
            kd_sc[c, ch] = (t["kf"] * jnp.exp(gc_last - gc)).astype(bf16)
            gl_sc[c, ch] = jnp.broadcast_to(jnp.exp(gc_last), (8, 128))
        return carry

    lax.fori_loop(0, GDN_NCH // GDN_PA, local, 0)

    def recur(i, carry):
        st = []
        for hl in range(GDN_HP):
            for d in range(2):
                if d == 0:
                    ch = jnp.where(i < GDN_CCH, SEQ // C + i, i - GDN_CCH)
                else:
                    ch = jnp.where(i < GDN_CCH, SEQ // C + GDN_CCH - 1 - i, GDN_NCH - 1 - i)
                st.append(dict(c=hl * 2 + d, hl=hl, ch=ch))
        for t in st:
            c, ch = t["c"], t["ch"]
            t["s"] = s_ref[c]
            t["wq"] = jnp.dot(wq_sc[c, ch], t["s"].astype(bf16), preferred_element_type=f32)
        for t in st:
            c, ch = t["c"], t["ch"]
            wq = t.pop("wq")
            v_new = (u_sc[c, ch].astype(f32) - wq[:C]).astype(bf16)
            o = wq[C:] + jnp.dot(at_sc[c, ch], v_new, preferred_element_type=f32)
            s_ref[c] = t.pop("s") * gl_sc[c, ch][0:1, :] + lax.dot_general(
                kd_sc[c, ch], v_new, (((0,), (0,)), ((), ())), preferred_element_type=f32)
            r0 = pl.multiple_of(ch * C, C)
            o_ref[0, pl.ds(r0, C), slice(t["hl"] * GDN_DV, (t["hl"] + 1) * GDN_DV)] += o
        return carry

    lax.fori_loop(0, GDN_NCH, recur, 0)


def _gdn_chunk(q, k, v, gb):
    wide = GDN_HP * GDN_DK
    nchain = 2 * GDN_HP
    blk = pl.BlockSpec((1, T, wide), lambda b, hp: (b, 0, hp))
    return pl.pallas_call(
        _gdn_chunk_kernel,
        grid=(B, GDN_HEADS // GDN_HP),
        in_specs=[blk, blk, blk, pl.BlockSpec((1, T, 128), lambda b, hp: (b, 0, 0))],
        out_specs=blk,
        out_shape=jax.ShapeDtypeStruct((B, T, D), f32),
        scratch_shapes=[
            pltpu.VMEM((nchain, GDN_DK, GDN_DV), f32),
            pltpu.VMEM((nchain, GDN_NCH, 2 * GDN_CHUNK, GDN_DK), bf16),
            pltpu.VMEM((nchain, GDN_NCH, GDN_CHUNK, GDN_DK), bf16),
            pltpu.VMEM((nchain, GDN_NCH, GDN_CHUNK, GDN_DV), bf16),
            pltpu.VMEM((nchain, GDN_NCH, GDN_CHUNK, GDN_CHUNK), bf16),
            pltpu.VMEM((nchain, GDN_NCH, 8, 128), f32),
        ],
        compiler_params=pltpu.CompilerParams(
            dimension_semantics=("arbitrary", "arbitrary"), vmem_limit_bytes=V7X_VMEM_LIMIT),
        name="gdn_chunk",
    )(q, k, v, gb)


def _gdn_out_kernel(o_ref, z_ref, ng_ref, w_ref, y_ref):
    parts = []
    for h in range(GDN_HEADS):
        cols = slice(h * GDN_DV, (h + 1) * GDN_DV)
        oh = o_ref[0, :, cols]
        zz = z_ref[0, :, cols].astype(f32)
        yn = oh * lax.rsqrt(jnp.mean(oh * oh, axis=1, keepdims=True) + EPS) * ng_ref[...]
        parts.append((yn * (zz * jax.nn.sigmoid(zz))).astype(bf16))
    y_ref[0] = jnp.dot(jnp.concatenate(parts, axis=1), w_ref[...], preferred_element_type=f32)


def _gdn_out(o, z, norm_g, w_out):
    tile = pl.BlockSpec((1, GDN_TM, D), lambda b, j: (b, j, 0))
    return pl.pallas_call(
        _gdn_out_kernel,
        grid=(B, GDN_NT),
        in_specs=[tile, tile, pl.BlockSpec((1, GDN_DV), lambda b, j: (0, 0)),
                  pl.BlockSpec((D, D), lambda b, j: (0, 0))],
        out_specs=tile,
        out_shape=jax.ShapeDtypeStruct((B, T, D), f32),
        compiler_params=pltpu.CompilerParams(
            dimension_semantics=("arbitrary", "arbitrary"), vmem_limit_bytes=V7X_VMEM_LIMIT),
        name="gdn_out",
    )(o, z, norm_g, w_out)


def _seq_order(h):
    rows = SEQ // GRID_W
    lat = h[:, :SEQ].reshape(B, rows, GRID_W, D).transpose(0, 2, 1, 3).reshape(B, SEQ, D)
    return jnp.concatenate([lat, h[:, SEQ:]], axis=1)


def _token_order(y):
    rows = SEQ // GRID_W
    lat = y[:, :SEQ].reshape(B, GRID_W, rows, D).transpose(0, 2, 1, 3).reshape(B, SEQ, D)
    return jnp.concatenate([lat, y[:, SEQ:]], axis=1)


def _gdn_mixer(h, w_in, conv_w, a_log, dt_bias, norm_g, w_out):
    q, k, v, z, gb = _gdn_proj(_seq_order(h), w_in, conv_w, a_log, dt_bias)
    o = _gdn_chunk(q, k, v, gb)
    return _token_order(_gdn_out(o, z, norm_g[None], w_out.astype(bf16)))


def kernel(x, c, ctx, c_ctx, mod_w, mod_b, norm_g, ffn_w13, ffn_w2, s5_a_re, s5_a_im, s5_log_dt,
           s5_b_re, s5_b_im, s5_c_re, s5_c_im, s5_d, s5_glu_w, s5_glu_b, gdn_w_in, gdn_conv_w,
           gdn_a_log, gdn_dt_bias, gdn_norm_g, gdn_w_out, final_g):
    tok = jnp.concatenate([x, ctx], axis=1)
    mods = _modulation(c, c_ctx, mod_w, mod_b).reshape(DEPTH, MOD_ROWS, N_MOD, D)
    w13 = ffn_w13.astype(bf16)
    w2 = ffn_w2.astype(bf16)
    for i in range(DEPTH):
        m = mods[i]
        tok, h = _ffn(tok, m[:, 0:3], norm_g[i, 0][None], w13[i, 0], w2[i, 0],
                      next_norm=(m[:, 3:6], norm_g[i, 1][None]))
        j = i // 2
        if i % 2 == 0:
            y = _s5_unflatten(_s5_core(_s5_flatten(h), s5_a_re[j], s5_a_im[j], s5_log_dt[j], s5_b_re[j],
                                       s5_b_im[j], s5_c_re[j], s5_c_im[j], s5_d[j]))
            tok = _glu_residual(tok, y, m[:, 3:6], s5_glu_w[j].astype(bf16), s5_glu_b[j][None])
            tok = _ffn(tok, m[:, 6:9], norm_g[i, 2][None], w13[i, 1], w2[i, 1])
        else:
            y = _gdn_mixer(h, gdn_w_in[j], gdn_conv_w[j], gdn_a_log[j], gdn_dt_bias[j], gdn_norm_g[j],
                           gdn_w_out[j])
            tok = _ffn(tok, m[:, 6:9], norm_g[i, 2][None], w13[i, 1], w2[i, 1], mixer_out=y, m3_mixer=m[:, 3:6])
    return _final_norm(tok, final_g[None])
```

```python
import functools

import jax
import jax.numpy as jnp
from jax import lax
from jax.experimental import pallas as pl
from jax.experimental.pallas import tpu as pltpu

D = 1024
B = 4
SEQ = 4096
CTX = 256
T = SEQ + CTX
DEPTH = 4
GRID_W = 64
N_MOD = 9
D_FF = 2816
S5_GROUP = 16
S5_GROUPS = D // S5_GROUP
S5_STATE = 64
GDN_HEADS = 8
GDN_DK = D // GDN_HEADS
GDN_DV = D // GDN_HEADS
GDN_CONV = 5
GDN_CHUNK = 128
EPS = 1e-6

MOD_ROWS = 8
V7X_VMEM_LIMIT = 56 * 1024 * 1024

f32 = jnp.float32
bf16 = jnp.bfloat16


MOD_TN = 2304


def _mod_kernel(c_ref, w_ref, b_ref, o_ref):
    c = c_ref[...]
    s = c * jax.nn.sigmoid(c)
    o_ref[0] = jnp.dot(s, w_ref[0], preferred_element_type=f32,
                       precision=lax.Precision.HIGHEST) + b_ref[0]


def _modulation(c, c_ctx, mod_w, mod_b):
    cc = jnp.zeros((MOD_ROWS, D), f32).at[:B].set(c).at[B].set(c_ctx)
    n = N_MOD * D
    return pl.pallas_call(
        _mod_kernel,
        grid=(DEPTH, n // MOD_TN),
        in_specs=[
            pl.BlockSpec((MOD_ROWS, D), lambda i, j: (0, 0)),
            pl.BlockSpec((1, D, MOD_TN), lambda i, j: (i, 0, j)),
            pl.BlockSpec((1, 1, MOD_TN), lambda i, j: (i, 0, j)),
        ],
        out_specs=pl.BlockSpec((1, MOD_ROWS, MOD_TN), lambda i, j: (i, 0, j)),
        out_shape=jax.ShapeDtypeStruct((DEPTH, MOD_ROWS, n), f32),
        compiler_params=pltpu.CompilerParams(
            dimension_semantics=("arbitrary", "arbitrary"), vmem_limit_bytes=V7X_VMEM_LIMIT),
        name="modulation",
    )(cc, mod_w, mod_b.reshape(DEPTH, 1, n))


def _row_modulation(m_ref, b, t, tm):
    rows = t * tm + lax.broadcasted_iota(jnp.int32, (tm, 1), 0)
    is_ctx = rows >= SEQ
    mx = m_ref[b]
    mc = m_ref[B]
    return tuple(jnp.where(is_ctx, mc[j:j + 1], mx[j:j + 1]) for j in range(3))


def _ada_norm(x, g, shift, scale):
    ms = jnp.mean(x * x, axis=-1, keepdims=True)
    return x * lax.rsqrt(ms + EPS) * g * (1.0 + scale) + shift


FFN_TM = 1088
FFN_CHUNK = 256


def _ffn_kernel(*refs, with_mixer_out, with_next_norm):
    refs = list(refs)
    if with_mixer_out:
        mm_ref, y_ref = refs[:2]
        refs = refs[2:]
    m_ref, g_ref, x_ref, w13_ref, w2_ref = refs[:5]
    refs = refs[5:]
    if with_next_norm:
        mn_ref, gn_ref, o_ref, h_ref = refs
    else:
        (o_ref,) = refs
    b = pl.program_id(0)
    t = pl.program_id(1)
    x = x_ref[0]
    if with_mixer_out:
        _, _, gate_mix = _row_modulation(mm_ref, b, t, FFN_TM)
        x = x + gate_mix * y_ref[0]
    shift, scale, gate = _row_modulation(m_ref, b, t, FFN_TM)
    h = _ada_norm(x, g_ref[...], shift, scale).astype(bf16)
    acc = jnp.zeros((FFN_TM, D), f32)
    for j in range(D_FF // FFN_CHUNK):
        lo = j * FFN_CHUNK
        a = jnp.dot(h, w13_ref[:, lo:lo + FFN_CHUNK], preferred_element_type=f32)
        u = jnp.dot(h, w13_ref[:, D_FF + lo:D_FF + lo + FFN_CHUNK], preferred_element_type=f32)
        hid = (a * jax.nn.sigmoid(a) * u).astype(bf16)
        acc = acc + jnp.dot(hid, w2_ref[lo:lo + FFN_CHUNK, :], preferred_element_type=f32)
    out = x + 0.5 * gate * acc
    o_ref[0] = out
    if with_next_norm:
        shift_n, scale_n, _ = _row_modulation(mn_ref, b, t, FFN_TM)
        h_ref[0] = _ada_norm(out, gn_ref[...], shift_n, scale_n).astype(bf16)


def _ffn(tok, m3, g, w13, w2, mixer_out=None, m3_mixer=None, next_norm=None):
    mod = pl.BlockSpec((MOD_ROWS, 3, D), lambda b, t: (0, 0, 0))
    gain = pl.BlockSpec((1, D), lambda b, t: (0, 0))
    tile = pl.BlockSpec((1, FFN_TM, D), lambda b, t: (b, t, 0))
    in_specs = [
        mod, gain, tile,
        pl.BlockSpec((D, 2 * D_FF), lambda b, t: (0, 0), pipeline_mode=pl.Buffered(1)),
        pl.BlockSpec((D_FF, D), lambda b, t: (0, 0), pipeline_mode=pl.Buffered(1)),
    ]
    args = (m3, g, tok, w13, w2)
    out_specs = tile
    out_shape = jax.ShapeDtypeStruct((B, T, D), f32)
    if mixer_out is not None:
        in_specs = [mod, tile] + in_specs
        args = (m3_mixer, mixer_out) + args
    if next_norm is not None:
        in_specs = in_specs + [mod, gain]
        args = args + tuple(next_norm)
        out_specs = [tile, tile]
        out_shape = [out_shape, jax.ShapeDtypeStruct((B, T, D), bf16)]
    return pl.pallas_call(
        functools.partial(_ffn_kernel, with_mixer_out=mixer_out is not None,
                          with_next_norm=next_norm is not None),
        grid=(B, T // FFN_TM),
        in_specs=in_specs,
        out_specs=out_specs,
        out_shape=out_shape,
        compiler_params=pltpu.CompilerParams(
            dimension_semantics=("arbitrary", "arbitrary"), vmem_limit_bytes=V7X_VMEM_LIMIT),
        name="ffn",
    )(*args)


FINAL_TM = 512


def _final_kernel(g_ref, x_ref, o_ref):
    x = x_ref[0]
    ms = jnp.mean(x * x, axis=-1, keepdims=True)
    o_ref[0] = x * lax.rsqrt(ms + EPS) * g_ref[...]


def _final_norm(tok, g):
    return pl.pallas_call(
        _final_kernel,
        grid=(B, SEQ // FINAL_TM),
        in_specs=[
            pl.BlockSpec((1, D), lambda b, t: (0, 0)),
            pl.BlockSpec((1, FINAL_TM, D), lambda b, t: (b, t, 0)),
        ],
        out_specs=pl.BlockSpec((1, FINAL_TM, D), lambda b, t: (b, t, 0)),
        out_shape=jax.ShapeDtypeStruct((B, SEQ, D), f32),
        compiler_params=pltpu.CompilerParams(dimension_semantics=("arbitrary", "arbitrary")),
        name="final_norm",
    )(g, tok)


S5_L = 64
S5_NCH = T // S5_L
S5_CCH = CTX // S5_L
S5_ROWS = S5_NCH * B
S5_W = S5_L * S5_GROUP
S5_BASE = 8


def _cmul(ar, ai, br, bi):
    return ar * br - ai * bi, ar * bi + ai * br


def _dot_nt(a, b):
    return lax.dot_general(a, b, (((1,), (1,)), ((), ())), preferred_element_type=f32)


def _s5_kernel(u_ref, are_ref, aim_ref, ldt_ref, btre_ref, btim_ref, cre_ref, cim_ref, dt_ref, y_ref,
               m_ref, s_ref, hf_ref, hb_ref):
    P = S5_STATE
    lane = lax.broadcasted_iota(jnp.int32, (1, 2 * P), 1)
    is_f = lane < P
    mf = is_f.astype(f32)
    mb = 1.0 - mf

    a_re = are_ref[0]
    a_im = aim_ref[0]
    dt = jnp.exp(ldt_ref[0])
    mag = jnp.exp(dt * a_re)
    ang = dt * a_im
    l_re = mag * jnp.cos(ang)
    l_im = mag * jnp.sin(ang)
    den = a_re * a_re + a_im * a_im
    n_re = l_re - 1.0
    f_re = (n_re * a_re + l_im * a_im) / den
    f_im = (l_im * a_re - n_re * a_im) / den

    c_re = cre_ref[0]
    c_im = cim_ref[0]
    a_r, a_i = _cmul(f_re, f_im, btre_ref[0], btim_ref[0])
    r_r, r_i = _cmul(l_re, l_im, c_re, c_im)
    p_re, p_im = l_re, l_im
    for _ in range(6):
        fm_re = jnp.where(is_f, p_re, 1.0)
        fm_im = jnp.where(is_f, p_im, 0.0)
        bm_re = jnp.where(is_f, 1.0, p_re)
        bm_im = jnp.where(is_f, 0.0, p_im)
        at_r, at_i = _cmul(fm_re, fm_im, a_r, a_i)
        ab_r, ab_i = _cmul(bm_re, bm_im, a_r, a_i)
        rt_r, rt_i = _cmul(bm_re, bm_im, r_r, r_i)
        rb_r, rb_i = _cmul(fm_re, fm_im, r_r, r_i)
        a_r = jnp.concatenate([at_r, ab_r], axis=0)
        a_i = jnp.concatenate([at_i, ab_i], axis=0)
        r_r = jnp.concatenate([rt_r, rb_r], axis=0)
        r_i = jnp.concatenate([rt_i, rb_i], axis=0)
        p_re, p_im = _cmul(p_re, p_im, p_re, p_im)
    acat = jnp.concatenate([a_r, a_i], axis=1).astype(bf16)
    rcat = jnp.concatenate([r_r, -r_i], axis=1)
    mf2 = jnp.concatenate([mf, mf], axis=1)
    rtf = (rcat * mf2).astype(bf16)
    rtb = (rcat * (1.0 - mf2)).astype(bf16)
    rcat = rcat.astype(bf16)

    nb = S5_BASE
    ccat = jnp.concatenate([c_re, -c_im], axis=1)
    ctf = jnp.concatenate([(ccat * mf2).astype(bf16)] * nb, axis=0)
    ctb = jnp.concatenate([(ccat * (1.0 - mf2)).astype(bf16)] * nb, axis=0)
    bw = nb * S5_GROUP
    kf_all = _dot_nt(acat[S5_W - bw:], ctf)
    kb_all = _dot_nt(acat[:bw], ctb)
    lane_t0 = lax.broadcasted_iota(jnp.int32, (S5_GROUP, bw), 1) // S5_GROUP
    lane_i = lax.broadcasted_iota(jnp.int32, (S5_GROUP, bw), 1)
    row_i = lax.broadcasted_iota(jnp.int32, (S5_GROUP, bw), 0)
    dtile = dt_ref[0]
    blocks = []
    for s0 in range(nb):
        acc = jnp.where(lane_i == s0 * S5_GROUP + row_i, dtile, 0.0)
        for lag in range(nb - s0):
            acc = acc + jnp.where(lane_t0 == s0 + lag, kf_all[(nb - 1 - lag) * S5_GROUP:(nb - lag) * S5_GROUP], 0.0)
        for lag in range(s0 + 1):
            acc = acc + jnp.where(lane_t0 == s0 - lag, kb_all[lag * S5_GROUP:(lag + 1) * S5_GROUP], 0.0)
        blocks.append(acc)
    blk0 = jnp.concatenate(blocks, axis=0).astype(bf16)
    for i in range(S5_L // nb):
        m_ref[i * bw:(i + 1) * bw, i * bw:(i + 1) * bw] = blk0
    n = nb
    while n < S5_L:
        w = n * S5_GROUP
        xn = _dot_nt(acat[S5_W - w:], rtf[:w]).astype(bf16)
        yn = _dot_nt(acat[:w], rtb[S5_W - w:]).astype(bf16)
        for j in range(S5_L // (2 * n)):
            o = 2 * j * w
            m_ref[o:o + w, o + w:o + 2 * w] = xn
            m_ref[o + w:o + 2 * w, o:o + w] = yn
        n *= 2

    u = u_ref[0]
    y = jnp.dot(u, m_ref[...], preferred_element_type=f32)
    s_ref[...] = jnp.dot(u, acat, preferred_element_type=f32)

    h_re = jnp.zeros((B, 2 * P), f32)
    h_im = jnp.zeros((B, 2 * P), f32)
    for j in range(S5_NCH):
        lat = S5_NCH - S5_CCH
        kf = lat + j if j < S5_CCH else j - S5_CCH
        kb = lat + S5_CCH - 1 - j if j < S5_CCH else S5_NCH - 1 - j
        hf_ref[kf * B:(kf + 1) * B, :] = jnp.concatenate([h_re, h_im], axis=1)
        hb_ref[kb * B:(kb + 1) * B, :] = jnp.concatenate([h_re, h_im], axis=1)
        s_re = jnp.where(is_f, s_ref[kf * B:(kf + 1) * B, 0:2 * P], s_ref[kb * B:(kb + 1) * B, 0:2 * P])
        s_im = jnp.where(is_f, s_ref[kf * B:(kf + 1) * B, 2 * P:4 * P], s_ref[kb * B:(kb + 1) * B, 2 * P:4 * P])
        h_re, h_im = (p_re * h_re - p_im * h_im + s_re, p_re * h_im + p_im * h_re + s_im)
    hcat = jnp.where(mf2 > 0.5, hf_ref[...], hb_ref[...]).astype(bf16)
    y_ref[0] = y + _dot_nt(hcat, rcat)


def _s5_core(uflat, a_re, a_im, log_dt, b_re, b_im, c_re, c_im, d):
    G = S5_GROUPS

    def pack(t):
        return jnp.concatenate([t[0], t[1]], axis=-1)

    are = pack(a_re)[:, None, :]
    aim = pack(a_im)[:, None, :]
    ldt = pack(jnp.broadcast_to(log_dt[:, :, None], (2, G, S5_STATE)))[:, None, :]
    btre = pack(jnp.swapaxes(b_re, -1, -2))
    btim = pack(jnp.swapaxes(b_im, -1, -2))
    cre = pack(c_re)
    cim = pack(c_im)
    dtile = jnp.tile(d.reshape(G, 1, S5_GROUP), (1, 1, S5_BASE))
    vec = pl.BlockSpec((1, 1, 2 * S5_STATE), lambda g: (g, 0, 0))
    mat = pl.BlockSpec((1, S5_GROUP, 2 * S5_STATE), lambda g: (g, 0, 0))
    big = pl.BlockSpec((1, S5_ROWS, S5_W), lambda g: (g, 0, 0))
    return pl.pallas_call(
        _s5_kernel,
        grid=(G,),
        in_specs=[big, vec, vec, vec, mat, mat, mat, mat, vec],
        out_specs=big,
        out_shape=jax.ShapeDtypeStruct((G, S5_ROWS, S5_W), f32),
        scratch_shapes=[
            pltpu.VMEM((S5_W, S5_W), bf16),
            pltpu.VMEM((S5_ROWS, 4 * S5_STATE), f32),
            pltpu.VMEM((S5_ROWS, 4 * S5_STATE), f32),
            pltpu.VMEM((S5_ROWS, 4 * S5_STATE), f32),
        ],
        compiler_params=pltpu.CompilerParams(
            dimension_semantics=("arbitrary",), vmem_limit_bytes=V7X_VMEM_LIMIT),
        name="s5_core",
    )(uflat, are, aim, ldt, btre, btim, cre, cim, dtile)


def _s5_flatten(h):
    hs = h.reshape(B, S5_NCH, S5_L, S5_GROUPS, S5_GROUP).transpose(3, 1, 0, 2, 4)
    return hs.reshape(S5_GROUPS, S5_ROWS, S5_W).astype(bf16)


def _s5_unflatten(y):
    return y.reshape(S5_GROUPS, S5_NCH, B, S5_L, S5_GROUP).transpose(2, 1, 3, 0, 4).reshape(B, T, D)


GLU_TM = 544


def _glu_kernel(m_ref, y_ref, x_ref, w_ref, b_ref, o_ref):
    b = pl.program_id(0)
    t = pl.program_id(1)
    _, _, gate = _row_modulation(m_ref, b, t, GLU_TM)
    z = jax.nn.gelu(y_ref[0]).astype(bf16)
    p = jnp.dot(z, w_ref[...], preferred_element_type=f32) + b_ref[...]
    o_ref[0] = x_ref[0] + gate * (p[:, :D] * jax.nn.sigmoid(p[:, D:]))


def _glu_residual(tok, y, m3, glu_w, glu_b):
    tile = pl.BlockSpec((1, GLU_TM, D), lambda b, t: (b, t, 0))
    return pl.pallas_call(
        _glu_kernel,
        grid=(B, T // GLU_TM),
        in_specs=[
            pl.BlockSpec((MOD_ROWS, 3, D), lambda b, t: (0, 0, 0)),
            tile, tile,
            pl.BlockSpec((D, 2 * D), lambda b, t: (0, 0)),
            pl.BlockSpec((1, 2 * D), lambda b, t: (0, 0)),
        ],
        out_specs=tile,
        out_shape=jax.ShapeDtypeStruct((B, T, D), f32),
        compiler_params=pltpu.CompilerParams(
            dimension_semantics=("arbitrary", "arbitrary"), vmem_limit_bytes=V7X_VMEM_LIMIT),
        name="glu_residual",
    )(m3, y, tok, glu_w, glu_b)


GDN_TM = 256
GDN_NT = T // GDN_TM
GRID_ROWS = SEQ // GRID_W
GDN_WT = GDN_TM // GRID_ROWS
GDN_CB = 16
GDN_CT = GDN_CB // GDN_WT
assert CTX == GDN_TM and SEQ % GDN_TM == 0 and GRID_W % GDN_CB == 0
GDN_HALO = 16
GDN_QKV = 3 * D
GDN_NPROJ = 4 * D + 128
GDN_NCH = T // GDN_CHUNK
GDN_CCH = CTX // GDN_CHUNK
GDN_HP = 2


def _gdn_proj_kernel(main_ref, prev_ref, next_ref, ctx_ref, w_ref, cw_ref, alog_ref, dtb_ref,
                     q_ref, k_ref, v_ref, z_ref, gb_ref, p_ref, hbuf):
    j = pl.program_id(1)
    pad = GDN_CONV // 2
    lo = GDN_HALO
    hi = GDN_HALO + GDN_TM
    top = GRID_ROWS - GDN_HALO

    for jm in range(GDN_CT):
        @pl.when((j < GDN_NT - 1) & (j % GDN_CT == jm))
        def _(jm=jm):
            w0 = jm * GDN_WT
            for r in range(GDN_WT):
                hbuf[lo + r * GRID_ROWS:lo + (r + 1) * GRID_ROWS, :] = main_ref[0, :, w0 + r, :]
            hbuf[0:lo, :] = main_ref[0, top:, w0 - 1, :] if jm > 0 else prev_ref[0, :, GDN_CB - 1, :]
            hbuf[hi:hi + GDN_HALO, :] = (main_ref[0, 0:GDN_HALO, w0 + GDN_WT, :] if jm < GDN_CT - 1
                                         else next_ref[0, :, 0, :])

    @pl.when(j == GDN_NT - 1)
    def _():
        hbuf[lo:hi, :] = ctx_ref[0].reshape(GDN_TM, D)

    @pl.when((j == 0) | (j == GDN_NT - 1))
    def _():
        hbuf[0:lo, :] = jnp.zeros((GDN_HALO, D), bf16)

    @pl.when(j >= GDN_NT - 2)
    def _():
        hbuf[hi:hi + GDN_HALO, :] = jnp.zeros((GDN_HALO, D), bf16)

    hm = hbuf[lo:hi, :]
    p_ref[...] = jnp.dot(hbuf[...], w_ref[:, 0:GDN_QKV], preferred_element_type=f32)

    for cb in range(GDN_QKV // GDN_DK):
        cols = slice(cb * GDN_DK, (cb + 1) * GDN_DK)
        acc = cw_ref[0:1, cols] * p_ref[lo - pad:hi - pad, cols]
        for kk in range(1, GDN_CONV):
            acc = acc + cw_ref[kk:kk + 1, cols] * p_ref[lo - pad + kk:hi - pad + kk, cols]
        xa = acc * jax.nn.sigmoid(acc)
        if cb < 2 * GDN_HEADS:
            xa = xa * lax.rsqrt(jnp.sum(xa * xa, axis=1, keepdims=True) + EPS)
        if cb < GDN_HEADS:
            q_ref[0, :, cols] = (xa * (GDN_DK ** -0.5)).astype(bf16)
        elif cb < 2 * GDN_HEADS:
            k_ref[0, :, slice((cb - GDN_HEADS) * GDN_DK, (cb - GDN_HEADS + 1) * GDN_DK)] = xa.astype(bf16)
        else:
            v_ref[0, :, slice((cb - 2 * GDN_HEADS) * GDN_DV, (cb - 2 * GDN_HEADS + 1) * GDN_DV)] = xa.astype(bf16)

    z_ref[0] = jnp.dot(hm, w_ref[:, GDN_QKV:4 * D], preferred_element_type=f32).astype(bf16)

    ba = jnp.dot(hm, w_ref[:, 4 * D:GDN_NPROJ], preferred_element_type=f32)
    xs = ba + dtb_ref[...]
    softplus = jnp.maximum(xs, 0.0) + jnp.log(1.0 + jnp.exp(-jnp.abs(xs)))
    g = -jnp.exp(alog_ref[...]) * softplus
    ri = lax.broadcasted_iota(jnp.int32, (GDN_TM, GDN_TM), 0)
    ci = lax.broadcasted_iota(jnp.int32, (GDN_TM, GDN_TM), 1)
    same = (ri // GDN_CHUNK) == (ci // GDN_CHUNK)
    tri_f = (same & (ci <= ri)).astype(f32)
    tri_b = (same & (ci >= ri)).astype(f32)
    gc_f = jnp.dot(tri_f, g, preferred_element_type=f32, precision=lax.Precision.HIGHEST)
    gc_b = jnp.dot(tri_b, g, preferred_element_type=f32, precision=lax.Precision.HIGHEST)
    lane = lax.broadcasted_iota(jnp.int32, (GDN_TM, 128), 1)
    gb_ref[0] = jnp.where(lane < GDN_HEADS, gc_f,
                          jnp.where(lane < 2 * GDN_HEADS, gc_b,
                                    jnp.where(lane < 4 * GDN_HEADS, jax.nn.sigmoid(ba), 0.0)))


def _gdn_proj(h, w_in, conv_w, a_log, dt_bias):
    nh = 2 * GDN_HEADS
    w = jnp.concatenate([w_in[:, :4 * D], w_in[:, 4 * D + nh:], w_in[:, 4 * D:4 * D + nh],
                         jnp.zeros((D, 128 - 2 * nh), f32)], axis=1).astype(bf16)
    alog = jnp.zeros((1, 128), f32).at[0, :nh].set(a_log.reshape(nh))
    dtb = jnp.zeros((1, 128), f32).at[0, :nh].set(dt_bias.reshape(nh))
    tile = pl.BlockSpec((1, GDN_TM, D), lambda b, j: (b, j, 0))
    rows = GDN_TM + 2 * GDN_HALO
    ncb = GRID_W // GDN_CB
    h4 = h.reshape(B, T // GRID_W, GRID_W, D)

    def cblock(j):
        return jnp.minimum(j // GDN_CT, ncb - 1)

    return pl.pallas_call(
        _gdn_proj_kernel,
        grid=(B, GDN_NT),
        in_specs=[
            pl.BlockSpec((1, GRID_ROWS, GDN_CB, D), lambda b, j: (b, 0, cblock(j), 0)),
            pl.BlockSpec((1, GDN_HALO, GDN_CB, D),
                         lambda b, j: (b, GRID_ROWS // GDN_HALO - 1, jnp.maximum(cblock(j) - 1, 0), 0)),
            pl.BlockSpec((1, GDN_HALO, GDN_CB, D), lambda b, j: (b, 0, jnp.minimum(cblock(j) + 1, ncb - 1), 0)),
            pl.BlockSpec((1, CTX // GRID_W, GRID_W, D), lambda b, j: (b, GRID_ROWS // (CTX // GRID_W), 0, 0)),
            pl.BlockSpec((D, GDN_NPROJ), lambda b, j: (0, 0), pipeline_mode=pl.Buffered(1)),
            pl.BlockSpec((GDN_CONV, GDN_QKV), lambda b, j: (0, 0)),
            pl.BlockSpec((1, 128), lambda b, j: (0, 0)),
            pl.BlockSpec((1, 128), lambda b, j: (0, 0)),
        ],
        out_specs=[tile, tile, tile, tile, pl.BlockSpec((1, GDN_TM, 128), lambda b, j: (b, j, 0))],
        out_shape=[jax.ShapeDtypeStruct((B, T, D), bf16)] * 4 + [jax.ShapeDtypeStruct((B, T, 128), f32)],
        scratch_shapes=[pltpu.VMEM((rows, GDN_QKV), f32), pltpu.VMEM((rows, D), bf16)],
        compiler_params=pltpu.CompilerParams(
            dimension_semantics=("arbitrary", "arbitrary"), vmem_limit_bytes=V7X_VMEM_LIMIT),
        name="gdn_proj",
    )(h4, h4, h4, h4, w, conv_w, alog, dtb)


GDN_PA = 2
GDN_INV_BASE = 8


def _gdn_chunk_kernel(q_ref, k_ref, v_ref, gb_ref, o_ref, s_ref, wq_sc, kd_sc, u_sc, at_sc, gl_sc):
    hp = pl.program_id(1)
    C = GDN_CHUNK
    nchain = 2 * GDN_HP
    o_ref[...] = jnp.zeros_like(o_ref)
    s_ref[...] = jnp.zeros_like(s_ref)
    ri = lax.broadcasted_iota(jnp.int32, (C, C), 0)
    ci = lax.broadcasted_iota(jnp.int32, (C, C), 1)
    eye = (ri == ci).astype(f32)
    lane = lax.broadcasted_iota(jnp.int32, (C, 128), 1)
    ones3 = jnp.ones((C, 3 * 128), bf16)
    masks = ((ci <= ri, ci < ri, C - 1), (ci >= ri, ci > ri, 0))
    diag_blk = (ri // GDN_INV_BASE) == (ci // GDN_INV_BASE)
    merge_masks = []
    n = GDN_INV_BASE
    while n < C:
        merge_masks.append(((ri // (2 * n)) == (ci // (2 * n))) & ((ri // n) != (ci // n)))
        n *= 2

    def local(m, carry):
        items = [(hl, d, e) for hl in range(GDN_HP) for d in range(2) for e in range(GDN_PA)]
        st = []
        for hl, d, e in items:
            ch = m * GDN_PA + e
            r0 = pl.multiple_of(ch * C, C)
            cols = slice(hl * GDN_DK, (hl + 1) * GDN_DK)
            q = q_ref[0, pl.ds(r0, C), cols]
            k = k_ref[0, pl.ds(r0, C), cols]
            v = v_ref[0, pl.ds(r0, C), cols].astype(f32)
            gbt = gb_ref[0, pl.ds(r0, C), :]
            lg = d * GDN_HEADS + hp * GDN_HP + hl
            gsel = jnp.where(lane == lg, gbt, 0.0)
            gc = jnp.sum(gsel, axis=1, keepdims=True)
            beta = jnp.sum(jnp.where(lane == 2 * GDN_HEADS + lg, gbt, 0.0), axis=1, keepdims=True)
            a1 = gsel.astype(bf16)
            r1 = gsel - a1.astype(f32)
            a2 = r1.astype(bf16)
            a3 = (r1 - a2.astype(f32)).astype(bf16)
            kf = k.astype(f32)
            kb = kf * beta
            st.append(dict(c=hl * 2 + d, d=d, ch=ch, q=q, k=k, v=v, gc=gc, beta=beta, kf=kf, kb=kb,
                           a3=jnp.concatenate([a1, a2, a3], axis=1)))
        for t in st:
            t["gc_row"] = _dot_nt(ones3, t.pop("a3"))
            t["kq"] = _dot_nt(jnp.concatenate([t["kb"].astype(bf16), t["q"]], axis=0), t["k"])
        for t in st:
            incl, strict, last = masks[t["d"]]
            decay = jnp.exp(jnp.where(incl, t["gc"] - t.pop("gc_row"), -jnp.inf))
            kq = t.pop("kq")
            a_mat = jnp.where(strict, kq[:C] * decay, 0.0)
            at_sc[t["c"], t["ch"]] = jnp.where(incl, kq[C:] * decay, 0.0).astype(bf16)
            t["a"] = a_mat
            x = -jnp.where(diag_blk, a_mat, 0.0)
            t["r"] = eye + x
            t["xb"] = x.astype(bf16)
        for t in st:
            xb = t.pop("xb")
            t["y"] = jnp.dot(xb, xb, preferred_element_type=f32)
        for t in st:
            yb = t.pop("y").astype(bf16)
            ry = jnp.dot(jnp.concatenate([t["r"].astype(bf16), yb], axis=0), yb, preferred_element_type=f32)
            t["r"] = t["r"] + ry[:C]
            t["y"] = ry[C:]
        for t in st:
            t["r"] = t["r"] + jnp.dot(t["r"].astype(bf16), t.pop("y").astype(bf16), preferred_element_type=f32)
        for off in merge_masks:
            for t in st:
                t["w"] = jnp.dot(jnp.where(off, t["a"], 0.0).astype(bf16), t["r"].astype(bf16),
                                 preferred_element_type=f32)
            for t in st:
                t["r"] = t["r"] - jnp.dot(t["r"].astype(bf16), t.pop("w").astype(bf16), preferred_element_type=f32)
        for t in st:
            incl, strict, last = masks[t["d"]]
            gc = t["gc"]
            gexp = jnp.exp(gc)
            rhs = jnp.concatenate([t["v"] * t["beta"], t["kb"] * gexp], axis=1)
            uw = rhs + jnp.dot((t["r"] - eye).astype(bf16), rhs.astype(bf16), preferred_element_type=f32)
            gc_last = gc[last:last + 1]
            c, ch = t["c"], t["ch"]
            u_sc[c, ch] = uw[:, :GDN_DV].astype(bf16)
            wq_sc[c, ch, 0:C, :] = uw[:, GDN_DV:].astype(bf16)
            wq_sc[c, ch, C:2 * C, :] = (t["q"].astype(f32) * gexp).astype(bf16)
            kd_sc[c, ch] = (t["kf"] * jnp.exp(gc_last - gc)).astype(bf16)
            gl_sc[c, ch] = jnp.broadcast_to(jnp.exp(gc_last), (8, 128))
        return carry

    lax.fori_loop(0, GDN_NCH // GDN_PA, local, 0)

    def recur(i, carry):
        st = []
        for hl in range(GDN_HP):
            for d in range(2):
                if d == 0:
                    ch = jnp.where(i < GDN_CCH, SEQ // C + i, i - GDN_CCH)
                else:
                    ch = jnp.where(i < GDN_CCH, SEQ // C + GDN_CCH - 1 - i, GDN_NCH - 1 - i)
                st.append(dict(c=hl * 2 + d, hl=hl, ch=ch))
        for t in st:
            c, ch = t["c"], t["ch"]
            t["s"] = s_ref[c]
            t["wq"] = jnp.dot(wq_sc[c, ch], t["s"].astype(bf16), preferred_element_type=f32)
        for t in st:
            c, ch = t["c"], t["ch"]
            wq = t.pop("wq")
            v_new = (u_sc[c, ch].astype(f32) - wq[:C]).astype(bf16)
            o = wq[C:] + jnp.dot(at_sc[c, ch], v_new, preferred_element_type=f32)
            s_ref[c] = t.pop("s") * gl_sc[c, ch][0:1, :] + lax.dot_general(
                kd_sc[c, ch], v_new, (((0,), (0,)), ((), ())), preferred_element_type=f32)
            r0 = pl.multiple_of(ch * C, C)
            o_ref[0, pl.ds(r0, C), slice(t["hl"] * GDN_DV, (t["hl"] + 1) * GDN_DV)] += o
        return carry

    lax.fori_loop(0, GDN_NCH, recur, 0)


def _gdn_chunk(q, k, v, gb):
    wide = GDN_HP * GDN_DK
    nchain = 2 * GDN_HP
    blk = pl.BlockSpec((1, T, wide), lambda b, hp: (b, 0, hp))
    return pl.pallas_call(
        _gdn_chunk_kernel,
        grid=(B, GDN_HEADS // GDN_HP),
        in_specs=[blk, blk, blk, pl.BlockSpec((1, T, 128), lambda b, hp: (b, 0, 0))],
        out_specs=blk,
        out_shape=jax.ShapeDtypeStruct((B, T, D), f32),
        scratch_shapes=[
            pltpu.VMEM((nchain, GDN_DK, GDN_DV), f32),
            pltpu.VMEM((nchain, GDN_NCH, 2 * GDN_CHUNK, GDN_DK), bf16),
            pltpu.VMEM((nchain, GDN_NCH, GDN_CHUNK, GDN_DK), bf16),
            pltpu.VMEM((nchain, GDN_NCH, GDN_CHUNK, GDN_DV), bf16),
            pltpu.VMEM((nchain, GDN_NCH, GDN_CHUNK, GDN_CHUNK), bf16),
            pltpu.VMEM((nchain, GDN_NCH, 8, 128), f32),
        ],
        compiler_params=pltpu.CompilerParams(
            dimension_semantics=("arbitrary", "arbitrary"), vmem_limit_bytes=V7X_VMEM_LIMIT),
        name="gdn_chunk",
    )(q, k, v, gb)


def _gdn_out_kernel(o_ref, z_ref, ng_ref, w_ref, y_ref):
    parts = []
    for h in range(GDN_HEADS):
        cols = slice(h * GDN_DV, (h + 1) * GDN_DV)
        oh = o_ref[0, :, cols]
        zz = z_ref[0, :, cols].astype(f32)
        yn = oh * lax.rsqrt(jnp.mean(oh * oh, axis=1, keepdims=True) + EPS) * ng_ref[...]
        parts.append((yn * (zz * jax.nn.sigmoid(zz))).astype(bf16))
    y_ref[0] = jnp.dot(jnp.concatenate(parts, axis=1), w_ref[...], preferred_element_type=f32)


def _gdn_out(o, z, norm_g, w_out):
    tile = pl.BlockSpec((1, GDN_TM, D), lambda b, j: (b, j, 0))
    return pl.pallas_call(
        _gdn_out_kernel,
        grid=(B, GDN_NT),
        in_specs=[tile, tile, pl.BlockSpec((1, GDN_DV), lambda b, j: (0, 0)),
                  pl.BlockSpec((D, D), lambda b, j: (0, 0))],
        out_specs=tile,
        out_shape=jax.ShapeDtypeStruct((B, T, D), f32),
        compiler_params=pltpu.CompilerParams(
            dimension_semantics=("arbitrary", "arbitrary"), vmem_limit_bytes=V7X_VMEM_LIMIT),
        name="gdn_out",
    )(o, z, norm_g, w_out)


def _token_order(y):
    lat = y[:, :SEQ].reshape(B, GRID_W, GRID_ROWS, D).transpose(0, 2, 1, 3).reshape(B, SEQ, D)
    return jnp.concatenate([lat, y[:, SEQ:]], axis=1)


def _gdn_mixer(h, w_in, conv_w, a_log, dt_bias, norm_g, w_out):
    q, k, v, z, gb = _gdn_proj(h, w_in, conv_w, a_log, dt_bias)
    o = _gdn_chunk(q, k, v, gb)
    return _token_order(_gdn_out(o, z, norm_g[None], w_out.astype(bf16)))


def kernel(x, c, ctx, c_ctx, mod_w, mod_b, norm_g, ffn_w13, ffn_w2, s5_a_re, s5_a_im, s5_log_dt,
           s5_b_re, s5_b_im, s5_c_re, s5_c_im, s5_d, s5_glu_w, s5_glu_b, gdn_w_in, gdn_conv_w,
           gdn_a_log, gdn_dt_bias, gdn_norm_g, gdn_w_out, final_g):
    tok = jnp.concatenate([x, ctx], axis=1)
    mods = _modulation(c, c_ctx, mod_w, mod_b).reshape(DEPTH, MOD_ROWS, N_MOD, D)
    w13 = ffn_w13.astype(bf16)
    w2 = ffn_w2.astype(bf16)
    for i in range(DEPTH):
        m = mods[i]
        tok, h = _ffn(tok, m[:, 0:3], norm_g[i, 0][None], w13[i, 0], w2[i, 0],
                      next_norm=(m[:, 3:6], norm_g[i, 1][None]))
        j = i // 2
        if i % 2 == 0:
            y = _s5_unflatten(_s5_core(_s5_flatten(h), s5_a_re[j], s5_a_im[j], s5_log_dt[j], s5_b_re[j],
                                       s5_b_im[j], s5_c_re[j], s5_c_im[j], s5_d[j]))
            tok = _glu_residual(tok, y, m[:, 3:6], s5_glu_w[j].astype(bf16), s5_glu_b[j][None])
            tok = _ffn(tok, m[:, 6:9], norm_g[i, 2][None], w13[i, 1], w2[i, 1])
        else:
            y = _gdn_mixer(h, gdn_w_in[j], gdn_conv_w[j], gdn_a_log[j], gdn_dt_bias[j], gdn_norm_g[j],
                           gdn_w_out[j])
            tok = _ffn(tok, m[:, 6:9], norm_g[i, 2][None], w13[i, 1], w2[i, 1], mixer_out=y, m3_mixer=m[:, 3:6])
    return _final_norm(tok, final_g[None])
```

```python
import functools

import jax
import jax.numpy as jnp
from jax import lax
from jax.experimental import pallas as pl
from jax.experimental.pallas import tpu as pltpu

D = 1024
B = 4
SEQ = 4096
CTX = 256
T = SEQ + CTX
DEPTH = 4
GRID_W = 64
N_MOD = 9
D_FF = 2816
S5_GROUP = 16
S5_GROUPS = D // S5_GROUP
S5_STATE = 64
GDN_HEADS = 8
GDN_DK = D // GDN_HEADS
GDN_DV = D // GDN_HEADS
GDN_CONV = 5
GDN_CHUNK = 128
EPS = 1e-6

MOD_ROWS = 8
V7X_VMEM_LIMIT = 56 * 1024 * 1024

f32 = jnp.float32
bf16 = jnp.bfloat16


MOD_TN = 2304


def _mod_kernel(c_ref, w_ref, b_ref, o_ref):
    c = c_ref[...]
    s = c * jax.nn.sigmoid(c)
    o_ref[0] = jnp.dot(s, w_ref[0], preferred_element_type=f32,
                       precision=lax.Precision.HIGHEST) + b_ref[0]


def _modulation(c, c_ctx, mod_w, mod_b):
    cc = jnp.zeros((MOD_ROWS, D), f32).at[:B].set(c).at[B].set(c_ctx)
    n = N_MOD * D
    return pl.pallas_call(
        _mod_kernel,
        grid=(DEPTH, n // MOD_TN),
        in_specs=[
            pl.BlockSpec((MOD_ROWS, D), lambda i, j: (0, 0)),
            pl.BlockSpec((1, D, MOD_TN), lambda i, j: (i, 0, j)),
            pl.BlockSpec((1, 1, MOD_TN), lambda i, j: (i, 0, j)),
        ],
        out_specs=pl.BlockSpec((1, MOD_ROWS, MOD_TN), lambda i, j: (i, 0, j)),
        out_shape=jax.ShapeDtypeStruct((DEPTH, MOD_ROWS, n), f32),
        compiler_params=pltpu.CompilerParams(
            dimension_semantics=("arbitrary", "arbitrary"), vmem_limit_bytes=V7X_VMEM_LIMIT),
        name="modulation",
    )(cc, mod_w, mod_b.reshape(DEPTH, 1, n))


def _row_modulation(m_ref, b, t, tm):
    rows = t * tm + lax.broadcasted_iota(jnp.int32, (tm, 1), 0)
    is_ctx = rows >= SEQ
    mx = m_ref[b]
    mc = m_ref[B]
    return tuple(jnp.where(is_ctx, mc[j:j + 1], mx[j:j + 1]) for j in range(3))


def _ada_norm(x, g, shift, scale):
    ms = jnp.mean(x * x, axis=-1, keepdims=True)
    return x * lax.rsqrt(ms + EPS) * g * (1.0 + scale) + shift


FFN_TM = 1088
FFN_CHUNK = 256


def _ffn_kernel(*refs, with_mixer_out, with_next_norm):
    refs = list(refs)
    if with_mixer_out:
        mm_ref, y_ref = refs[:2]
        refs = refs[2:]
    m_ref, g_ref, x_ref, w13_ref, w2_ref = refs[:5]
    refs = refs[5:]
    if with_next_norm:
        mn_ref, gn_ref, o_ref, h_ref = refs
    else:
        (o_ref,) = refs
    b = pl.program_id(0)
    t = pl.program_id(1)
    x = x_ref[0]
    if with_mixer_out:
        _, _, gate_mix = _row_modulation(mm_ref, b, t, FFN_TM)
        x = x + gate_mix * y_ref[0].astype(f32)
    shift, scale, gate = _row_modulation(m_ref, b, t, FFN_TM)
    h = _ada_norm(x, g_ref[...], shift, scale).astype(bf16)
    acc = jnp.zeros((FFN_TM, D), f32)
    for j in range(D_FF // FFN_CHUNK):
        lo = j * FFN_CHUNK
        a = jnp.dot(h, w13_ref[:, lo:lo + FFN_CHUNK], preferred_element_type=f32)
        u = jnp.dot(h, w13_ref[:, D_FF + lo:D_FF + lo + FFN_CHUNK], preferred_element_type=f32)
        hid = (a * jax.nn.sigmoid(a) * u).astype(bf16)
        acc = acc + jnp.dot(hid, w2_ref[lo:lo + FFN_CHUNK, :], preferred_element_type=f32)
    out = x + 0.5 * gate * acc
    o_ref[0] = out
    if with_next_norm:
        shift_n, scale_n, _ = _row_modulation(mn_ref, b, t, FFN_TM)
        h_ref[0] = _ada_norm(out, gn_ref[...], shift_n, scale_n).astype(bf16)


def _ffn(tok, m3, g, w13, w2, mixer_out=None, m3_mixer=None, next_norm=None):
    mod = pl.BlockSpec((MOD_ROWS, 3, D), lambda b, t: (0, 0, 0))
    gain = pl.BlockSpec((1, D), lambda b, t: (0, 0))
    tile = pl.BlockSpec((1, FFN_TM, D), lambda b, t: (b, t, 0))
    in_specs = [
        mod, gain, tile,
        pl.BlockSpec((D, 2 * D_FF), lambda b, t: (0, 0), pipeline_mode=pl.Buffered(1)),
        pl.BlockSpec((D_FF, D), lambda b, t: (0, 0), pipeline_mode=pl.Buffered(1)),
    ]
    args = (m3, g, tok, w13, w2)
    out_specs = tile
    out_shape = jax.ShapeDtypeStruct((B, T, D), f32)
    if mixer_out is not None:
        in_specs = [mod, tile] + in_specs
        args = (m3_mixer, mixer_out) + args
    if next_norm is not None:
        in_specs = in_specs + [mod, gain]
        args = args + tuple(next_norm)
        out_specs = [tile, tile]
        out_shape = [out_shape, jax.ShapeDtypeStruct((B, T, D), bf16)]
    return pl.pallas_call(
        functools.partial(_ffn_kernel, with_mixer_out=mixer_out is not None,
                          with_next_norm=next_norm is not None),
        grid=(B, T // FFN_TM),
        in_specs=in_specs,
        out_specs=out_specs,
        out_shape=out_shape,
        compiler_params=pltpu.CompilerParams(
            dimension_semantics=("arbitrary", "arbitrary"), vmem_limit_bytes=V7X_VMEM_LIMIT),
        name="ffn",
    )(*args)


FINAL_TM = 512


def _final_kernel(g_ref, x_ref, o_ref):
    x = x_ref[0]
    ms = jnp.mean(x * x, axis=-1, keepdims=True)
    o_ref[0] = x * lax.rsqrt(ms + EPS) * g_ref[...]


def _final_norm(tok, g):
    return pl.pallas_call(
        _final_kernel,
        grid=(B, SEQ // FINAL_TM),
        in_specs=[
            pl.BlockSpec((1, D), lambda b, t: (0, 0)),
            pl.BlockSpec((1, FINAL_TM, D), lambda b, t: (b, t, 0)),
        ],
        out_specs=pl.BlockSpec((1, FINAL_TM, D), lambda b, t: (b, t, 0)),
        out_shape=jax.ShapeDtypeStruct((B, SEQ, D), f32),
        compiler_params=pltpu.CompilerParams(dimension_semantics=("arbitrary", "arbitrary")),
        name="final_norm",
    )(g, tok)


S5_L = 64
S5_NCH = T // S5_L
S5_CCH = CTX // S5_L
S5_ROWS = S5_NCH * B
S5_W = S5_L * S5_GROUP
S5_BASE = 8


def _cmul(ar, ai, br, bi):
    return ar * br - ai * bi, ar * bi + ai * br


def _dot_nt(a, b):
    return lax.dot_general(a, b, (((1,), (1,)), ((), ())), preferred_element_type=f32)


def _s5_kernel(u_ref, are_ref, aim_ref, ldt_ref, btre_ref, btim_ref, cre_ref, cim_ref, dt_ref, y_ref,
               m_ref, s_ref, hf_ref, hb_ref):
    P = S5_STATE
    lane = lax.broadcasted_iota(jnp.int32, (1, 2 * P), 1)
    is_f = lane < P
    mf = is_f.astype(f32)
    mb = 1.0 - mf

    a_re = are_ref[0]
    a_im = aim_ref[0]
    dt = jnp.exp(ldt_ref[0])
    mag = jnp.exp(dt * a_re)
    ang = dt * a_im
    l_re = mag * jnp.cos(ang)
    l_im = mag * jnp.sin(ang)
    den = a_re * a_re + a_im * a_im
    n_re = l_re - 1.0
    f_re = (n_re * a_re + l_im * a_im) / den
    f_im = (l_im * a_re - n_re * a_im) / den

    c_re = cre_ref[0]
    c_im = cim_ref[0]
    a_r, a_i = _cmul(f_re, f_im, btre_ref[0], btim_ref[0])
    r_r, r_i = _cmul(l_re, l_im, c_re, c_im)
    p_re, p_im = l_re, l_im
    for _ in range(6):
        fm_re = jnp.where(is_f, p_re, 1.0)
        fm_im = jnp.where(is_f, p_im, 0.0)
        bm_re = jnp.where(is_f, 1.0, p_re)
        bm_im = jnp.where(is_f, 0.0, p_im)
        at_r, at_i = _cmul(fm_re, fm_im, a_r, a_i)
        ab_r, ab_i = _cmul(bm_re, bm_im, a_r, a_i)
        rt_r, rt_i = _cmul(bm_re, bm_im, r_r, r_i)
        rb_r, rb_i = _cmul(fm_re, fm_im, r_r, r_i)
        a_r = jnp.concatenate([at_r, ab_r], axis=0)
        a_i = jnp.concatenate([at_i, ab_i], axis=0)
        r_r = jnp.concatenate([rt_r, rb_r], axis=0)
        r_i = jnp.concatenate([rt_i, rb_i], axis=0)
        p_re, p_im = _cmul(p_re, p_im, p_re, p_im)
    acat = jnp.concatenate([a_r, a_i], axis=1).astype(bf16)
    rcat = jnp.concatenate([r_r, -r_i], axis=1)
    mf2 = jnp.concatenate([mf, mf], axis=1)
    rtf = (rcat * mf2).astype(bf16)
    rtb = (rcat * (1.0 - mf2)).astype(bf16)
    rcat = rcat.astype(bf16)

    nb = S5_BASE
    ccat = jnp.concatenate([c_re, -c_im], axis=1)
    ctf = jnp.concatenate([(ccat * mf2).astype(bf16)] * nb, axis=0)
    ctb = jnp.concatenate([(ccat * (1.0 - mf2)).astype(bf16)] * nb, axis=0)
    bw = nb * S5_GROUP
    kf_all = _dot_nt(acat[S5_W - bw:], ctf)
    kb_all = _dot_nt(acat[:bw], ctb)
    lane_t0 = lax.broadcasted_iota(jnp.int32, (S5_GROUP, bw), 1) // S5_GROUP
    lane_i = lax.broadcasted_iota(jnp.int32, (S5_GROUP, bw), 1)
    row_i = lax.broadcasted_iota(jnp.int32, (S5_GROUP, bw), 0)
    dtile = dt_ref[0]
    blocks = []
    for s0 in range(nb):
        acc = jnp.where(lane_i == s0 * S5_GROUP + row_i, dtile, 0.0)
        for lag in range(nb - s0):
            acc = acc + jnp.where(lane_t0 == s0 + lag, kf_all[(nb - 1 - lag) * S5_GROUP:(nb - lag) * S5_GROUP], 0.0)
        for lag in range(s0 + 1):
            acc = acc + jnp.where(lane_t0 == s0 - lag, kb_all[lag * S5_GROUP:(lag + 1) * S5_GROUP], 0.0)
        blocks.append(acc)
    blk0 = jnp.concatenate(blocks, axis=0).astype(bf16)
    for i in range(S5_L // nb):
        m_ref[i * bw:(i + 1) * bw, i * bw:(i + 1) * bw] = blk0
    n = nb
    while n < S5_L:
        w = n * S5_GROUP
        xn = _dot_nt(acat[S5_W - w:], rtf[:w]).astype(bf16)
        yn = _dot_nt(acat[:w], rtb[S5_W - w:]).astype(bf16)
        for j in range(S5_L // (2 * n)):
            o = 2 * j * w
            m_ref[o:o + w, o + w:o + 2 * w] = xn
            m_ref[o + w:o + 2 * w, o:o + w] = yn
        n *= 2

    u = u_ref[0]
    y = jnp.dot(u, m_ref[...], preferred_element_type=f32)
    s_ref[...] = jnp.dot(u, acat, preferred_element_type=f32)

    h_re = jnp.zeros((B, 2 * P), f32)
    h_im = jnp.zeros((B, 2 * P), f32)
    for j in range(S5_NCH):
        lat = S5_NCH - S5_CCH
        kf = lat + j if j < S5_CCH else j - S5_CCH
        kb = lat + S5_CCH - 1 - j if j < S5_CCH else S5_NCH - 1 - j
        hf_ref[kf * B:(kf + 1) * B, :] = jnp.concatenate([h_re, h_im], axis=1)
        hb_ref[kb * B:(kb + 1) * B, :] = jnp.concatenate([h_re, h_im], axis=1)
        s_re = jnp.where(is_f, s_ref[kf * B:(kf + 1) * B, 0:2 * P], s_ref[kb * B:(kb + 1) * B, 0:2 * P])
        s_im = jnp.where(is_f, s_ref[kf * B:(kf + 1) * B, 2 * P:4 * P], s_ref[kb * B:(kb + 1) * B, 2 * P:4 * P])
        h_re, h_im = (p_re * h_re - p_im * h_im + s_re, p_re * h_im + p_im * h_re + s_im)
    hcat = jnp.where(mf2 > 0.5, hf_ref[...], hb_ref[...]).astype(bf16)
    y_ref[0] = (y + _dot_nt(hcat, rcat)).astype(bf16)


def _s5_core(uflat, a_re, a_im, log_dt, b_re, b_im, c_re, c_im, d):
    G = S5_GROUPS

    def pack(t):
        return jnp.concatenate([t[0], t[1]], axis=-1)

    are = pack(a_re)[:, None, :]
    aim = pack(a_im)[:, None, :]
    ldt = pack(jnp.broadcast_to(log_dt[:, :, None], (2, G, S5_STATE)))[:, None, :]
    btre = pack(jnp.swapaxes(b_re, -1, -2))
    btim = pack(jnp.swapaxes(b_im, -1, -2))
    cre = pack(c_re)
    cim = pack(c_im)
    dtile = jnp.tile(d.reshape(G, 1, S5_GROUP), (1, 1, S5_BASE))
    vec = pl.BlockSpec((1, 1, 2 * S5_STATE), lambda g: (g, 0, 0))
    mat = pl.BlockSpec((1, S5_GROUP, 2 * S5_STATE), lambda g: (g, 0, 0))
    big = pl.BlockSpec((1, S5_ROWS, S5_W), lambda g: (g, 0, 0))
    return pl.pallas_call(
        _s5_kernel,
        grid=(G,),
        in_specs=[big, vec, vec, vec, mat, mat, mat, mat, vec],
        out_specs=big,
        out_shape=jax.ShapeDtypeStruct((G, S5_ROWS, S5_W), bf16),
        scratch_shapes=[
            pltpu.VMEM((S5_W, S5_W), bf16),
            pltpu.VMEM((S5_ROWS, 4 * S5_STATE), f32),
            pltpu.VMEM((S5_ROWS, 4 * S5_STATE), f32),
            pltpu.VMEM((S5_ROWS, 4 * S5_STATE), f32),
        ],
        compiler_params=pltpu.CompilerParams(
            dimension_semantics=("arbitrary",), vmem_limit_bytes=V7X_VMEM_LIMIT),
        name="s5_core",
    )(uflat, are, aim, ldt, btre, btim, cre, cim, dtile)


def _s5_flatten(h):
    hs = h.reshape(B, S5_NCH, S5_L, S5_GROUPS, S5_GROUP).transpose(3, 1, 0, 2, 4)
    return hs.reshape(S5_GROUPS, S5_ROWS, S5_W).astype(bf16)


def _s5_unflatten(y):
    return y.reshape(S5_GROUPS, S5_NCH, B, S5_L, S5_GROUP).transpose(2, 1, 3, 0, 4).reshape(B, T, D)


GLU_TM = 544


def _glu_kernel(m_ref, y_ref, x_ref, w_ref, b_ref, o_ref):
    b = pl.program_id(0)
    t = pl.program_id(1)
    _, _, gate = _row_modulation(m_ref, b, t, GLU_TM)
    z = jax.nn.gelu(y_ref[0].astype(f32)).astype(bf16)
    p = jnp.dot(z, w_ref[...], preferred_element_type=f32) + b_ref[...]
    o_ref[0] = x_ref[0] + gate * (p[:, :D] * jax.nn.sigmoid(p[:, D:]))


def _glu_residual(tok, y, m3, glu_w, glu_b):
    tile = pl.BlockSpec((1, GLU_TM, D), lambda b, t: (b, t, 0))
    return pl.pallas_call(
        _glu_kernel,
        grid=(B, T // GLU_TM),
        in_specs=[
            pl.BlockSpec((MOD_ROWS, 3, D), lambda b, t: (0, 0, 0)),
            tile, tile,
            pl.BlockSpec((D, 2 * D), lambda b, t: (0, 0)),
            pl.BlockSpec((1, 2 * D), lambda b, t: (0, 0)),
        ],
        out_specs=tile,
        out_shape=jax.ShapeDtypeStruct((B, T, D), f32),
        compiler_params=pltpu.CompilerParams(
            dimension_semantics=("arbitrary", "arbitrary"), vmem_limit_bytes=V7X_VMEM_LIMIT),
        name="glu_residual",
    )(m3, y, tok, glu_w, glu_b)


GDN_TM = 256
GDN_NT = T // GDN_TM
GRID_ROWS = SEQ // GRID_W
GDN_WT = GDN_TM // GRID_ROWS
GDN_CB = 16
GDN_CT = GDN_CB // GDN_WT
assert CTX == GDN_TM and SEQ % GDN_TM == 0 and GRID_W % GDN_CB == 0
GDN_HALO = 16
GDN_QKV = 3 * D
GDN_NPROJ = 4 * D + 128
GDN_NCH = T // GDN_CHUNK
GDN_CCH = CTX // GDN_CHUNK
GDN_HP = 2


def _gdn_proj_kernel(main_ref, prev_ref, next_ref, ctx_ref, w_ref, cw_ref, alog_ref, dtb_ref,
                     q_ref, k_ref, v_ref, z_ref, gb_ref, p_ref, hbuf):
    j = pl.program_id(1)
    pad = GDN_CONV // 2
    lo = GDN_HALO
    hi = GDN_HALO + GDN_TM
    top = GRID_ROWS - GDN_HALO

    for jm in range(GDN_CT):
        @pl.when((j < GDN_NT - 1) & (j % GDN_CT == jm))
        def _(jm=jm):
            w0 = jm * GDN_WT
            for r in range(GDN_WT):
                hbuf[lo + r * GRID_ROWS:lo + (r + 1) * GRID_ROWS, :] = main_ref[0, :, w0 + r, :]
            hbuf[0:lo, :] = main_ref[0, top:, w0 - 1, :] if jm > 0 else prev_ref[0, :, GDN_CB - 1, :]
            hbuf[hi:hi + GDN_HALO, :] = (main_ref[0, 0:GDN_HALO, w0 + GDN_WT, :] if jm < GDN_CT - 1
                                         else next_ref[0, :, 0, :])

    @pl.when(j == GDN_NT - 1)
    def _():
        hbuf[lo:hi, :] = ctx_ref[0].reshape(GDN_TM, D)

    @pl.when((j == 0) | (j == GDN_NT - 1))
    def _():
        hbuf[0:lo, :] = jnp.zeros((GDN_HALO, D), bf16)

    @pl.when(j >= GDN_NT - 2)
    def _():
        hbuf[hi:hi + GDN_HALO, :] = jnp.zeros((GDN_HALO, D), bf16)

    hm = hbuf[lo:hi, :]
    p_ref[...] = jnp.dot(hbuf[...], w_ref[:, 0:GDN_QKV], preferred_element_type=f32)

    for cb in range(GDN_QKV // GDN_DK):
        cols = slice(cb * GDN_DK, (cb + 1) * GDN_DK)
        acc = cw_ref[0:1, cols] * p_ref[lo - pad:hi - pad, cols]
        for kk in range(1, GDN_CONV):
            acc = acc + cw_ref[kk:kk + 1, cols] * p_ref[lo - pad + kk:hi - pad + kk, cols]
        xa = acc * jax.nn.sigmoid(acc)
        if cb < 2 * GDN_HEADS:
            xa = xa * lax.rsqrt(jnp.sum(xa * xa, axis=1, keepdims=True) + EPS)
        if cb < GDN_HEADS:
            q_ref[0, :, cols] = (xa * (GDN_DK ** -0.5)).astype(bf16)
        elif cb < 2 * GDN_HEADS:
            k_ref[0, :, slice((cb - GDN_HEADS) * GDN_DK, (cb - GDN_HEADS + 1) * GDN_DK)] = xa.astype(bf16)
        else:
            v_ref[0, :, slice((cb - 2 * GDN_HEADS) * GDN_DV, (cb - 2 * GDN_HEADS + 1) * GDN_DV)] = xa.astype(bf16)

    z_ref[0] = jnp.dot(hm, w_ref[:, GDN_QKV:4 * D], preferred_element_type=f32).astype(bf16)

    ba = jnp.dot(hm, w_ref[:, 4 * D:GDN_NPROJ], preferred_element_type=f32)
    xs = ba + dtb_ref[...]
    softplus = jnp.maximum(xs, 0.0) + jnp.log(1.0 + jnp.exp(-jnp.abs(xs)))
    g = -jnp.exp(alog_ref[...]) * softplus
    ri = lax.broadcasted_iota(jnp.int32, (GDN_TM, GDN_TM), 0)
    ci = lax.broadcasted_iota(jnp.int32, (GDN_TM, GDN_TM), 1)
    same = (ri // GDN_CHUNK) == (ci // GDN_CHUNK)
    tri_f = (same & (ci <= ri)).astype(f32)
    tri_b = (same & (ci >= ri)).astype(f32)
    gc_f = jnp.dot(tri_f, g, preferred_element_type=f32, precision=lax.Precision.HIGHEST)
    gc_b = jnp.dot(tri_b, g, preferred_element_type=f32, precision=lax.Precision.HIGHEST)
    lane = lax.broadcasted_iota(jnp.int32, (GDN_TM, 128), 1)
    gb_ref[0] = jnp.where(lane < GDN_HEADS, gc_f,
                          jnp.where(lane < 2 * GDN_HEADS, gc_b,
                                    jnp.where(lane < 4 * GDN_HEADS, jax.nn.sigmoid(ba), 0.0)))


def _gdn_proj(h, w_in, conv_w, a_log, dt_bias):
    nh = 2 * GDN_HEADS
    w = jnp.concatenate([w_in[:, :4 * D], w_in[:, 4 * D + nh:], w_in[:, 4 * D:4 * D + nh],
                         jnp.zeros((D, 128 - 2 * nh), f32)], axis=1).astype(bf16)
    alog = jnp.zeros((1, 128), f32).at[0, :nh].set(a_log.reshape(nh))
    dtb = jnp.zeros((1, 128), f32).at[0, :nh].set(dt_bias.reshape(nh))
    tile = pl.BlockSpec((1, GDN_TM, D), lambda b, j: (b, j, 0))
    rows = GDN_TM + 2 * GDN_HALO
    ncb = GRID_W // GDN_CB
    h4 = h.reshape(B, T // GRID_W, GRID_W, D)

    def cblock(j):
        return jnp.minimum(j // GDN_CT, ncb - 1)

    return pl.pallas_call(
        _gdn_proj_kernel,
        grid=(B, GDN_NT),
        in_specs=[
            pl.BlockSpec((1, GRID_ROWS, GDN_CB, D), lambda b, j: (b, 0, cblock(j), 0)),
            pl.BlockSpec((1, GDN_HALO, GDN_CB, D),
                         lambda b, j: (b, GRID_ROWS // GDN_HALO - 1, jnp.maximum(cblock(j) - 1, 0), 0)),
            pl.BlockSpec((1, GDN_HALO, GDN_CB, D), lambda b, j: (b, 0, jnp.minimum(cblock(j) + 1, ncb - 1), 0)),
            pl.BlockSpec((1, CTX // GRID_W, GRID_W, D), lambda b, j: (b, GRID_ROWS // (CTX // GRID_W), 0, 0)),
            pl.BlockSpec((D, GDN_NPROJ), lambda b, j: (0, 0), pipeline_mode=pl.Buffered(1)),
            pl.BlockSpec((GDN_CONV, GDN_QKV), lambda b, j: (0, 0)),
            pl.BlockSpec((1, 128), lambda b, j: (0, 0)),
            pl.BlockSpec((1, 128), lambda b, j: (0, 0)),
        ],
        out_specs=[tile, tile, tile, tile, pl.BlockSpec((1, GDN_TM, 128), lambda b, j: (b, j, 0))],
        out_shape=[jax.ShapeDtypeStruct((B, T, D), bf16)] * 4 + [jax.ShapeDtypeStruct((B, T, 128), f32)],
        scratch_shapes=[pltpu.VMEM((rows, GDN_QKV), f32), pltpu.VMEM((rows, D), bf16)],
        compiler_params=pltpu.CompilerParams(
            dimension_semantics=("arbitrary", "arbitrary"), vmem_limit_bytes=V7X_VMEM_LIMIT),
        name="gdn_proj",
    )(h4, h4, h4, h4, w, conv_w, alog, dtb)


GDN_PA = 2
GDN_INV_BASE = 8


def _gdn_chunk_kernel(q_ref, k_ref, v_ref, gb_ref, o_ref, s_ref, wq_sc, kd_sc, u_sc, at_sc, gl_sc):
    hp = pl.program_id(1)
    C = GDN_CHUNK
    nchain = 2 * GDN_HP
    o_ref[...] = jnp.zeros_like(o_ref)
    s_ref[...] = jnp.zeros_like(s_ref)
    ri = lax.broadcasted_iota(jnp.int32, (C, C), 0)
    ci = lax.broadcasted_iota(jnp.int32, (C, C), 1)
    eye = (ri == ci).astype(f32)
    lane = lax.broadcasted_iota(jnp.int32, (C, 128), 1)
    ones3 = jnp.ones((C, 3 * 128), bf16)
    masks = ((ci <= ri, ci < ri, C - 1), (ci >= ri, ci > ri, 0))
    diag_blk = (ri // GDN_INV_BASE) == (ci // GDN_INV_BASE)
    merge_masks = []
    n = GDN_INV_BASE
    while n < C:
        merge_masks.append(((ri // (2 * n)) == (ci // (2 * n))) & ((ri // n) != (ci // n)))
        n *= 2

    def local(m, carry):
        items = [(hl, d, e) for hl in range(GDN_HP) for d in range(2) for e in range(GDN_PA)]
        st = []
        for hl, d, e in items:
            ch = m * GDN_PA + e
            r0 = pl.multiple_of(ch * C, C)
            cols = slice(hl * GDN_DK, (hl + 1) * GDN_DK)
            q = q_ref[0, pl.ds(r0, C), cols]
            k = k_ref[0, pl.ds(r0, C), cols]
            v = v_ref[0, pl.ds(r0, C), cols].astype(f32)
            gbt = gb_ref[0, pl.ds(r0, C), :]
            lg = d * GDN_HEADS + hp * GDN_HP + hl
            gsel = jnp.where(lane == lg, gbt, 0.0)
            gc = jnp.sum(gsel, axis=1, keepdims=True)
            beta = jnp.sum(jnp.where(lane == 2 * GDN_HEADS + lg, gbt, 0.0), axis=1, keepdims=True)
            a1 = gsel.astype(bf16)
            r1 = gsel - a1.astype(f32)
            a2 = r1.astype(bf16)
            a3 = (r1 - a2.astype(f32)).astype(bf16)
            kf = k.astype(f32)
            kb = kf * beta
            st.append(dict(c=hl * 2 + d, d=d, ch=ch, q=q, k=k, v=v, gc=gc, beta=beta, kf=kf, kb=kb,
                           a3=jnp.concatenate([a1, a2, a3], axis=1)))
        for t in st:
            t["gc_row"] = _dot_nt(ones3, t.pop("a3"))
            t["kq"] = _dot_nt(jnp.concatenate([t["kb"].astype(bf16), t["q"]], axis=0), t["k"])
        for t in st:
            incl, strict, last = masks[t["d"]]
            decay = jnp.exp(jnp.where(incl, t["gc"] - t.pop("gc_row"), -jnp.inf))
            kq = t.pop("kq")
            a_mat = jnp.where(strict, kq[:C] * decay, 0.0)
            at_sc[t["c"], t["ch"]] = jnp.where(incl, kq[C:] * decay, 0.0).astype(bf16)
            t["a"] = a_mat.astype(bf16)
            x = -jnp.where(diag_blk, a_mat, 0.0)
            t["r"] = eye + x
            t["xb"] = x.astype(bf16)
        for t in st:
            xb = t.pop("xb")
            t["y"] = jnp.dot(xb, xb, preferred_element_type=f32)
        for t in st:
            yb = t.pop("y").astype(bf16)
            ry = jnp.dot(jnp.concatenate([t["r"].astype(bf16), yb], axis=0), yb, preferred_element_type=f32)
            t["r"] = t["r"] + ry[:C]
            t["y"] = ry[C:]
        for t in st:
            t["r"] = t["r"] + jnp.dot(t["r"].astype(bf16), t.pop("y").astype(bf16), preferred_element_type=f32)
        for off in merge_masks:
            for t in st:
                t["w"] = jnp.dot(jnp.where(off, t["a"], jnp.zeros_like(t["a"])), t["r"].astype(bf16),
                                 preferred_element_type=f32)
            for t in st:
                t["r"] = t["r"] - jnp.dot(t["r"].astype(bf16), t.pop("w").astype(bf16), preferred_element_type=f32)
        for t in st:
            incl, strict, last = masks[t["d"]]
            gc = t["gc"]
            gexp = jnp.exp(gc)
            rhs = jnp.concatenate([t["v"] * t["beta"], t["kb"] * gexp], axis=1)
            uw = rhs + jnp.dot((t["r"] - eye).astype(bf16), rhs.astype(bf16), preferred_element_type=f32)
            gc_last = gc[last:last + 1]
            c, ch = t["c"], t["ch"]
            u_sc[c, ch] = uw[:, :GDN_DV].astype(bf16)
            wq_sc[c, ch, 0:C, :] = uw[:, GDN_DV:].astype(bf16)
            wq_sc[c, ch, C:2 * C, :] = (t["q"].astype(f32) * gexp).astype(bf16)
            kd_sc[c, ch] = (t["kf"] * jnp.exp(gc_last - gc)).astype(bf16)
            gl_sc[c, ch] = jnp.broadcast_to(jnp.exp(gc_last), (8, 128))
        return carry

    lax.fori_loop(0, GDN_NCH // GDN_PA, local, 0)

    def recur(i, carry):
        st = []
        for hl in range(GDN_HP):
            for d in range(2):
                if d == 0:
                    ch = jnp.where(i < GDN_CCH, SEQ // C + i, i - GDN_CCH)
                else:
                    ch = jnp.where(i < GDN_CCH, SEQ // C + GDN_CCH - 1 - i, GDN_NCH - 1 - i)
                st.append(dict(c=hl * 2 + d, hl=hl, ch=ch))
        for t in st:
            c, ch = t["c"], t["ch"]
            t["s"] = s_ref[c]
            t["wq"] = jnp.dot(wq_sc[c, ch], t["s"].astype(bf16), preferred_element_type=f32)
        for t in st:
            c, ch = t["c"], t["ch"]
            wq = t.pop("wq")
            v_new = (u_sc[c, ch].astype(f32) - wq[:C]).astype(bf16)
            o = wq[C:] + jnp.dot(at_sc[c, ch], v_new, preferred_element_type=f32)
            s_ref[c] = t.pop("s") * gl_sc[c, ch][0:1, :] + lax.dot_general(
                kd_sc[c, ch], v_new, (((0,), (0,)), ((), ())), preferred_element_type=f32)
            r0 = pl.multiple_of(ch * C, C)
            o_ref[0, pl.ds(r0, C), slice(t["hl"] * GDN_DV, (t["hl"] + 1) * GDN_DV)] += o
        return carry

    lax.fori_loop(0, GDN_NCH, recur, 0)


def _gdn_chunk(q, k, v, gb):
    wide = GDN_HP * GDN_DK
    nchain = 2 * GDN_HP
    blk = pl.BlockSpec((1, T, wide), lambda b, hp: (b, 0, hp))
    return pl.pallas_call(
        _gdn_chunk_kernel,
        grid=(B, GDN_HEADS // GDN_HP),
        in_specs=[blk, blk, blk, pl.BlockSpec((1, T, 128), lambda b, hp: (b, 0, 0))],
        out_specs=blk,
        out_shape=jax.ShapeDtypeStruct((B, T, D), f32),
        scratch_shapes=[
            pltpu.VMEM((nchain, GDN_DK, GDN_DV), f32),
            pltpu.VMEM((nchain, GDN_NCH, 2 * GDN_CHUNK, GDN_DK), bf16),
            pltpu.VMEM((nchain, GDN_NCH, GDN_CHUNK, GDN_DK), bf16),
            pltpu.VMEM((nchain, GDN_NCH, GDN_CHUNK, GDN_DV), bf16),
            pltpu.VMEM((nchain, GDN_NCH, GDN_CHUNK, GDN_CHUNK), bf16),
            pltpu.VMEM((nchain, GDN_NCH, 8, 128), f32),
        ],
        compiler_params=pltpu.CompilerParams(
            dimension_semantics=("arbitrary", "arbitrary"), vmem_limit_bytes=V7X_VMEM_LIMIT),
        name="gdn_chunk",
    )(q, k, v, gb)


def _gdn_out_kernel(o_ref, z_ref, ng_ref, w_ref, y_ref):
    parts = []
    for h in range(GDN_HEADS):
        cols = slice(h * GDN_DV, (h + 1) * GDN_DV)
        oh = o_ref[0, :, cols]
        zz = z_ref[0, :, cols].astype(f32)
        yn = oh * lax.rsqrt(jnp.mean(oh * oh, axis=1, keepdims=True) + EPS) * ng_ref[...]
        parts.append((yn * (zz * jax.nn.sigmoid(zz))).astype(bf16))
    y_ref[0] = jnp.dot(jnp.concatenate(parts, axis=1), w_ref[...], preferred_element_type=f32).astype(bf16)


def _gdn_out(o, z, norm_g, w_out):
    tile = pl.BlockSpec((1, GDN_TM, D), lambda b, j: (b, j, 0))
    return pl.pallas_call(
        _gdn_out_kernel,
        grid=(B, GDN_NT),
        in_specs=[tile, tile, pl.BlockSpec((1, GDN_DV), lambda b, j: (0, 0)),
                  pl.BlockSpec((D, D), lambda b, j: (0, 0))],
        out_specs=tile,
        out_shape=jax.ShapeDtypeStruct((B, T, D), bf16),
        compiler_params=pltpu.CompilerParams(
            dimension_semantics=("arbitrary", "arbitrary"), vmem_limit_bytes=V7X_VMEM_LIMIT),
        name="gdn_out",
    )(o, z, norm_g, w_out)


def _token_order(y):
    lat = y[:, :SEQ].reshape(B, GRID_W, GRID_ROWS, D).transpose(0, 2, 1, 3).reshape(B, SEQ, D)
    return jnp.concatenate([lat, y[:, SEQ:]], axis=1)


def _gdn_mixer(h, w_in, conv_w, a_log, dt_bias, norm_g, w_out):
    q, k, v, z, gb = _gdn_proj(h, w_in, conv_w, a_log, dt_bias)
    o = _gdn_chunk(q, k, v, gb)
    return _token_order(_gdn_out(o, z, norm_g[None], w_out.astype(bf16)))


def kernel(x, c, ctx, c_ctx, mod_w, mod_b, norm_g, ffn_w13, ffn_w2, s5_a_re, s5_a_im, s5_log_dt,
           s5_b_re, s5_b_im, s5_c_re, s5_c_im, s5_d, s5_glu_w, s5_glu_b, gdn_w_in, gdn_conv_w,
           gdn_a_log, gdn_dt_bias, gdn_norm_g, gdn_w_out, final_g):
    tok = jnp.concatenate([x, ctx], axis=1)
    mods = _modulation(c, c_ctx, mod_w, mod_b).reshape(DEPTH, MOD_ROWS, N_MOD, D)
    w13 = ffn_w13.astype(bf16)
    w2 = ffn_w2.astype(bf16)
    for i in range(DEPTH):
        m = mods[i]
        tok, h = _ffn(tok, m[:, 0:3], norm_g[i, 0][None], w13[i, 0], w2[i, 0],
                      next_norm=(m[:, 3:6], norm_g[i, 1][None]))
        j = i // 2
        if i % 2 == 0:
            y = _s5_unflatten(_s5_core(_s5_flatten(h), s5_a_re[j], s5_a_im[j], s5_log_dt[j], s5_b_re[j],
                                       s5_b_im[j], s5_c_re[j], s5_c_im[j], s5_d[j]))
            tok = _glu_residual(tok, y, m[:, 3:6], s5_glu_w[j].astype(bf16), s5_glu_b[j][None])
            tok = _ffn(tok, m[:, 6:9], norm_g[i, 2][None], w13[i, 1], w2[i, 1])
        else:
            y = _gdn_mixer(h, gdn_w_in[j], gdn_conv_w[j], gdn_a_log[j], gdn_dt_bias[j], gdn_norm_g[j],
                           gdn_w_out[j])
            tok = _ffn(tok, m[:, 6:9], norm_g[i, 2][None], w13[i, 1], w2[i, 1], mixer_out=y, m3_mixer=m[:, 3:6])
    return _final_norm(tok, final_g[None])
```

```python
import functools

import jax
import jax.numpy as jnp
from jax import lax
from jax.experimental import pallas as pl
from jax.experimental.pallas import tpu as pltpu

D = 1024
B = 4
SEQ = 4096
CTX = 256
T = SEQ + CTX
DEPTH = 4
GRID_W = 64
N_MOD = 9
D_FF = 2816
S5_GROUP = 16
S5_GROUPS = D // S5_GROUP
S5_STATE = 64
GDN_HEADS = 8
GDN_DK = D // GDN_HEADS
GDN_DV = D // GDN_HEADS
GDN_CONV = 5
GDN_CHUNK = 128
EPS = 1e-6

MOD_ROWS = 8
V7X_VMEM_LIMIT = 56 * 1024 * 1024

f32 = jnp.float32
bf16 = jnp.bfloat16


MOD_TN = 2304


def _mod_kernel(c_ref, w_ref, b_ref, o_ref):
    c = c_ref[...]
    s = c * jax.nn.sigmoid(c)
    o_ref[0] = jnp.dot(s, w_ref[0], preferred_element_type=f32,
                       precision=lax.Precision.HIGHEST) + b_ref[0]


def _modulation(c, c_ctx, mod_w, mod_b):
    cc = jnp.zeros((MOD_ROWS, D), f32).at[:B].set(c).at[B].set(c_ctx)
    n = N_MOD * D
    return pl.pallas_call(
        _mod_kernel,
        grid=(DEPTH, n // MOD_TN),
        in_specs=[
            pl.BlockSpec((MOD_ROWS, D), lambda i, j: (0, 0)),
            pl.BlockSpec((1, D, MOD_TN), lambda i, j: (i, 0, j)),
            pl.BlockSpec((1, 1, MOD_TN), lambda i, j: (i, 0, j)),
        ],
        out_specs=pl.BlockSpec((1, MOD_ROWS, MOD_TN), lambda i, j: (i, 0, j)),
        out_shape=jax.ShapeDtypeStruct((DEPTH, MOD_ROWS, n), f32),
        compiler_params=pltpu.CompilerParams(
            dimension_semantics=("arbitrary", "arbitrary"), vmem_limit_bytes=V7X_VMEM_LIMIT),
        name="modulation",
    )(cc, mod_w, mod_b.reshape(DEPTH, 1, n))


def _row_modulation(m_ref, b, t, tm):
    rows = t * tm + lax.broadcasted_iota(jnp.int32, (tm, 1), 0)
    is_ctx = rows >= SEQ
    mx = m_ref[b]
    mc = m_ref[B]
    return tuple(jnp.where(is_ctx, mc[j:j + 1], mx[j:j + 1]) for j in range(3))


def _ada_norm(x, g, shift, scale):
    ms = jnp.mean(x * x, axis=-1, keepdims=True)
    return x * lax.rsqrt(ms + EPS) * g * (1.0 + scale) + shift


FFN_TM = 1088
FFN_CHUNK = 256


def _ffn_kernel(*refs, with_mixer_out, with_next_norm):
    refs = list(refs)
    if with_mixer_out:
        mm_ref, y_ref = refs[:2]
        refs = refs[2:]
    m_ref, g_ref, x_ref, w13_ref, w2_ref = refs[:5]
    refs = refs[5:]
    if with_next_norm:
        mn_ref, gn_ref, o_ref, h_ref = refs
    else:
        (o_ref,) = refs
    b = pl.program_id(0)
    t = pl.program_id(1)
    x = x_ref[0]
    if with_mixer_out:
        _, _, gate_mix = _row_modulation(mm_ref, b, t, FFN_TM)
        x = x + gate_mix * y_ref[0].astype(f32)
    shift, scale, gate = _row_modulation(m_ref, b, t, FFN_TM)
    h = _ada_norm(x, g_ref[...], shift, scale).astype(bf16)
    acc = jnp.zeros((FFN_TM, D), f32)
    for j in range(D_FF // FFN_CHUNK):
        lo = j * FFN_CHUNK
        a = jnp.dot(h, w13_ref[:, lo:lo + FFN_CHUNK], preferred_element_type=f32)
        u = jnp.dot(h, w13_ref[:, D_FF + lo:D_FF + lo + FFN_CHUNK], preferred_element_type=f32)
        hid = (a * jax.nn.sigmoid(a) * u).astype(bf16)
        acc = acc + jnp.dot(hid, w2_ref[lo:lo + FFN_CHUNK, :], preferred_element_type=f32)
    out = x + 0.5 * gate * acc
    o_ref[0] = out
    if with_next_norm:
        shift_n, scale_n, _ = _row_modulation(mn_ref, b, t, FFN_TM)
        h_ref[0] = _ada_norm(out, gn_ref[...], shift_n, scale_n).astype(bf16)


def _ffn(tok, m3, g, w13, w2, layer, which, mixer_out=None, m3_mixer=None, next_norm=None):
    mod = pl.BlockSpec((MOD_ROWS, 3, D), lambda b, t: (0, 0, 0))
    gain = pl.BlockSpec((1, D), lambda b, t: (0, 0))
    tile = pl.BlockSpec((1, FFN_TM, D), lambda b, t: (b, t, 0))
    in_specs = [
        mod, gain, tile,
        pl.BlockSpec((None, None, D, 2 * D_FF), lambda b, t: (layer, which, 0, 0), pipeline_mode=pl.Buffered(1)),
        pl.BlockSpec((None, None, D_FF, D), lambda b, t: (layer, which, 0, 0), pipeline_mode=pl.Buffered(1)),
    ]
    args = (m3, g, tok, w13, w2)
    out_specs = tile
    out_shape = jax.ShapeDtypeStruct((B, T, D), f32)
    if mixer_out is not None:
        in_specs = [mod, tile] + in_specs
        args = (m3_mixer, mixer_out) + args
    if next_norm is not None:
        in_specs = in_specs + [mod, gain]
        args = args + tuple(next_norm)
        out_specs = [tile, tile]
        out_shape = [out_shape, jax.ShapeDtypeStruct((B, T, D), bf16)]
    return pl.pallas_call(
        functools.partial(_ffn_kernel, with_mixer_out=mixer_out is not None,
                          with_next_norm=next_norm is not None),
        grid=(B, T // FFN_TM),
        in_specs=in_specs,
        out_specs=out_specs,
        out_shape=out_shape,
        compiler_params=pltpu.CompilerParams(
            dimension_semantics=("arbitrary", "arbitrary"), vmem_limit_bytes=V7X_VMEM_LIMIT),
        name="ffn",
    )(*args)


FINAL_TM = 512


def _final_kernel(g_ref, x_ref, o_ref):
    x = x_ref[0]
    ms = jnp.mean(x * x, axis=-1, keepdims=True)
    o_ref[0] = x * lax.rsqrt(ms + EPS) * g_ref[...]


def _final_norm(tok, g):
    return pl.pallas_call(
        _final_kernel,
        grid=(B, SEQ // FINAL_TM),
        in_specs=[
            pl.BlockSpec((1, D), lambda b, t: (0, 0)),
            pl.BlockSpec((1, FINAL_TM, D), lambda b, t: (b, t, 0)),
        ],
        out_specs=pl.BlockSpec((1, FINAL_TM, D), lambda b, t: (b, t, 0)),
        out_shape=jax.ShapeDtypeStruct((B, SEQ, D), f32),
        compiler_params=pltpu.CompilerParams(dimension_semantics=("arbitrary", "arbitrary")),
        name="final_norm",
    )(g, tok)


S5_L = 64
S5_NCH = T // S5_L
S5_CCH = CTX // S5_L
S5_ROWS = S5_NCH * B
S5_W = S5_L * S5_GROUP
S5_BASE = 8


def _cmul(ar, ai, br, bi):
    return ar * br - ai * bi, ar * bi + ai * br


def _dot_nt(a, b):
    return lax.dot_general(a, b, (((1,), (1,)), ((), ())), preferred_element_type=f32)


def _s5_kernel(u_ref, are_ref, aim_ref, ldt_ref, btre_ref, btim_ref, cre_ref, cim_ref, dt_ref, y_ref,
               m_ref, s_ref, hf_ref, hb_ref):
    P = S5_STATE
    lane = lax.broadcasted_iota(jnp.int32, (1, 2 * P), 1)
    is_f = lane < P
    mf = is_f.astype(f32)
    mb = 1.0 - mf

    a_re = are_ref[0]
    a_im = aim_ref[0]
    dt = jnp.exp(ldt_ref[0])
    mag = jnp.exp(dt * a_re)
    ang = dt * a_im
    l_re = mag * jnp.cos(ang)
    l_im = mag * jnp.sin(ang)
    den = a_re * a_re + a_im * a_im
    n_re = l_re - 1.0
    f_re = (n_re * a_re + l_im * a_im) / den
    f_im = (l_im * a_re - n_re * a_im) / den

    c_re = cre_ref[0]
    c_im = cim_ref[0]
    a_r, a_i = _cmul(f_re, f_im, btre_ref[0], btim_ref[0])
    r_r, r_i = _cmul(l_re, l_im, c_re, c_im)
    p_re, p_im = l_re, l_im
    for _ in range(6):
        fm_re = jnp.where(is_f, p_re, 1.0)
        fm_im = jnp.where(is_f, p_im, 0.0)
        bm_re = jnp.where(is_f, 1.0, p_re)
        bm_im = jnp.where(is_f, 0.0, p_im)
        at_r, at_i = _cmul(fm_re, fm_im, a_r, a_i)
        ab_r, ab_i = _cmul(bm_re, bm_im, a_r, a_i)
        rt_r, rt_i = _cmul(bm_re, bm_im, r_r, r_i)
        rb_r, rb_i = _cmul(fm_re, fm_im, r_r, r_i)
        a_r = jnp.concatenate([at_r, ab_r], axis=0)
        a_i = jnp.concatenate([at_i, ab_i], axis=0)
        r_r = jnp.concatenate([rt_r, rb_r], axis=0)
        r_i = jnp.concatenate([rt_i, rb_i], axis=0)
        p_re, p_im = _cmul(p_re, p_im, p_re, p_im)
    acat = jnp.concatenate([a_r, a_i], axis=1).astype(bf16)
    rcat = jnp.concatenate([r_r, -r_i], axis=1)
    mf2 = jnp.concatenate([mf, mf], axis=1)
    rtf = (rcat * mf2).astype(bf16)
    rtb = (rcat * (1.0 - mf2)).astype(bf16)
    rcat = rcat.astype(bf16)

    nb = S5_BASE
    ccat = jnp.concatenate([c_re, -c_im], axis=1)
    ctf = jnp.concatenate([(ccat * mf2).astype(bf16)] * nb, axis=0)
    ctb = jnp.concatenate([(ccat * (1.0 - mf2)).astype(bf16)] * nb, axis=0)
    bw = nb * S5_GROUP
    kf_all = _dot_nt(acat[S5_W - bw:], ctf)
    kb_all = _dot_nt(acat[:bw], ctb)
    lane_t0 = lax.broadcasted_iota(jnp.int32, (S5_GROUP, bw), 1) // S5_GROUP
    lane_i = lax.broadcasted_iota(jnp.int32, (S5_GROUP, bw), 1)
    row_i = lax.broadcasted_iota(jnp.int32, (S5_GROUP, bw), 0)
    dtile = dt_ref[0]
    blocks = []
    for s0 in range(nb):
        acc = jnp.where(lane_i == s0 * S5_GROUP + row_i, dtile, 0.0)
        for lag in range(nb - s0):
            acc = acc + jnp.where(lane_t0 == s0 + lag, kf_all[(nb - 1 - lag) * S5_GROUP:(nb - lag) * S5_GROUP], 0.0)
        for lag in range(s0 + 1):
            acc = acc + jnp.where(lane_t0 == s0 - lag, kb_all[lag * S5_GROUP:(lag + 1) * S5_GROUP], 0.0)
        blocks.append(acc)
    blk0 = jnp.concatenate(blocks, axis=0).astype(bf16)
    for i in range(S5_L // nb):
        m_ref[i * bw:(i + 1) * bw, i * bw:(i + 1) * bw] = blk0
    n = nb
    while n < S5_L:
        w = n * S5_GROUP
        xn = _dot_nt(acat[S5_W - w:], rtf[:w]).astype(bf16)
        yn = _dot_nt(acat[:w], rtb[S5_W - w:]).astype(bf16)
        for j in range(S5_L // (2 * n)):
            o = 2 * j * w
            m_ref[o:o + w, o + w:o + 2 * w] = xn
            m_ref[o + w:o + 2 * w, o:o + w] = yn
        n *= 2

    u = u_ref[0]
    y = jnp.dot(u, m_ref[...], preferred_element_type=f32)
    s_ref[...] = jnp.dot(u, acat, preferred_element_type=f32)

    h_re = jnp.zeros((B, 2 * P), f32)
    h_im = jnp.zeros((B, 2 * P), f32)
    for j in range(S5_NCH):
        lat = S5_NCH - S5_CCH
        kf = lat + j if j < S5_CCH else j - S5_CCH
        kb = lat + S5_CCH - 1 - j if j < S5_CCH else S5_NCH - 1 - j
        hf_ref[kf * B:(kf + 1) * B, :] = jnp.concatenate([h_re, h_im], axis=1)
        hb_ref[kb * B:(kb + 1) * B, :] = jnp.concatenate([h_re, h_im], axis=1)
        s_re = jnp.where(is_f, s_ref[kf * B:(kf + 1) * B, 0:2 * P], s_ref[kb * B:(kb + 1) * B, 0:2 * P])
        s_im = jnp.where(is_f, s_ref[kf * B:(kf + 1) * B, 2 * P:4 * P], s_ref[kb * B:(kb + 1) * B, 2 * P:4 * P])
        h_re, h_im = (p_re * h_re - p_im * h_im + s_re, p_re * h_im + p_im * h_re + s_im)
    hcat = jnp.where(mf2 > 0.5, hf_ref[...], hb_ref[...]).astype(bf16)
    y_ref[0] = (y + _dot_nt(hcat, rcat)).astype(bf16)


def _s5_core(uflat, a_re, a_im, log_dt, b_re, b_im, c_re, c_im, d):
    G = S5_GROUPS

    def pack(t):
        return jnp.concatenate([t[0], t[1]], axis=-1)

    are = pack(a_re)[:, None, :]
    aim = pack(a_im)[:, None, :]
    ldt = pack(jnp.broadcast_to(log_dt[:, :, None], (2, G, S5_STATE)))[:, None, :]
    btre = pack(jnp.swapaxes(b_re, -1, -2))
    btim = pack(jnp.swapaxes(b_im, -1, -2))
    cre = pack(c_re)
    cim = pack(c_im)
    dtile = jnp.tile(d.reshape(G, 1, S5_GROUP), (1, 1, S5_BASE))
    vec = pl.BlockSpec((1, 1, 2 * S5_STATE), lambda g: (g, 0, 0))
    mat = pl.BlockSpec((1, S5_GROUP, 2 * S5_STATE), lambda g: (g, 0, 0))
    big = pl.BlockSpec((1, S5_ROWS, S5_W), lambda g: (g, 0, 0))
    return pl.pallas_call(
        _s5_kernel,
        grid=(G,),
        in_specs=[big, vec, vec, vec, mat, mat, mat, mat, vec],
        out_specs=big,
        out_shape=jax.ShapeDtypeStruct((G, S5_ROWS, S5_W), bf16),
        scratch_shapes=[
            pltpu.VMEM((S5_W, S5_W), bf16),
            pltpu.VMEM((S5_ROWS, 4 * S5_STATE), f32),
            pltpu.VMEM((S5_ROWS, 4 * S5_STATE), f32),
            pltpu.VMEM((S5_ROWS, 4 * S5_STATE), f32),
        ],
        compiler_params=pltpu.CompilerParams(
            dimension_semantics=("arbitrary",), vmem_limit_bytes=V7X_VMEM_LIMIT),
        name="s5_core",
    )(uflat, are, aim, ldt, btre, btim, cre, cim, dtile)


def _s5_flatten(h):
    hs = h.reshape(B, S5_NCH, S5_L, S5_GROUPS, S5_GROUP).transpose(3, 1, 0, 2, 4)
    return hs.reshape(S5_GROUPS, S5_ROWS, S5_W).astype(bf16)


def _s5_unflatten(y):
    return y.reshape(S5_GROUPS, S5_NCH, B, S5_L, S5_GROUP).transpose(2, 1, 3, 0, 4).reshape(B, T, D)


GLU_TM = 544


def _glu_kernel(m_ref, y_ref, x_ref, w_ref, b_ref, o_ref):
    b = pl.program_id(0)
    t = pl.program_id(1)
    _, _, gate = _row_modulation(m_ref, b, t, GLU_TM)
    z = jax.nn.gelu(y_ref[0].astype(f32)).astype(bf16)
    p = jnp.dot(z, w_ref[...], preferred_element_type=f32) + b_ref[...]
    o_ref[0] = x_ref[0] + gate * (p[:, :D] * jax.nn.sigmoid(p[:, D:]))


def _glu_residual(tok, y, m3, glu_w, glu_b):
    tile = pl.BlockSpec((1, GLU_TM, D), lambda b, t: (b, t, 0))
    return pl.pallas_call(
        _glu_kernel,
        grid=(B, T // GLU_TM),
        in_specs=[
            pl.BlockSpec((MOD_ROWS, 3, D), lambda b, t: (0, 0, 0)),
            tile, tile,
            pl.BlockSpec((D, 2 * D), lambda b, t: (0, 0)),
            pl.BlockSpec((1, 2 * D), lambda b, t: (0, 0)),
        ],
        out_specs=tile,
        out_shape=jax.ShapeDtypeStruct((B, T, D), f32),
        compiler_params=pltpu.CompilerParams(
            dimension_semantics=("arbitrary", "arbitrary"), vmem_limit_bytes=V7X_VMEM_LIMIT),
        name="glu_residual",
    )(m3, y, tok, glu_w, glu_b)


GDN_TM = 256
GDN_NT = T // GDN_TM
GRID_ROWS = SEQ // GRID_W
GDN_WT = GDN_TM // GRID_ROWS
GDN_CB = 16
GDN_CT = GDN_CB // GDN_WT
assert CTX == GDN_TM and SEQ % GDN_TM == 0 and GRID_W % GDN_CB == 0
GDN_HALO = 16
GDN_QKV = 3 * D
GDN_NPROJ = 4 * D + 128
GDN_NCH = T // GDN_CHUNK
GDN_CCH = CTX // GDN_CHUNK
GDN_HP = 2


def _gdn_proj_kernel(main_ref, prev_ref, next_ref, ctx_ref, w_ref, cw_ref, alog_ref, dtb_ref,
                     q_ref, k_ref, v_ref, z_ref, gb_ref, p_ref, hbuf):
    j = pl.program_id(1)
    pad = GDN_CONV // 2
    lo = GDN_HALO
    hi = GDN_HALO + GDN_TM
    top = GRID_ROWS - GDN_HALO

    for jm in range(GDN_CT):
        @pl.when((j < GDN_NT - 1) & (j % GDN_CT == jm))
        def _(jm=jm):
            w0 = jm * GDN_WT
            for r in range(GDN_WT):
                hbuf[lo + r * GRID_ROWS:lo + (r + 1) * GRID_ROWS, :] = main_ref[0, :, w0 + r, :]
            hbuf[0:lo, :] = main_ref[0, top:, w0 - 1, :] if jm > 0 else prev_ref[0, :, GDN_CB - 1, :]
            hbuf[hi:hi + GDN_HALO, :] = (main_ref[0, 0:GDN_HALO, w0 + GDN_WT, :] if jm < GDN_CT - 1
                                         else next_ref[0, :, 0, :])

    @pl.when(j == GDN_NT - 1)
    def _():
        hbuf[lo:hi, :] = ctx_ref[0].reshape(GDN_TM, D)

    @pl.when((j == 0) | (j == GDN_NT - 1))
    def _():
        hbuf[0:lo, :] = jnp.zeros((GDN_HALO, D), bf16)

    @pl.when(j >= GDN_NT - 2)
    def _():
        hbuf[hi:hi + GDN_HALO, :] = jnp.zeros((GDN_HALO, D), bf16)

    hm = hbuf[lo:hi, :]
    p_ref[...] = jnp.dot(hbuf[...], w_ref[:, 0:GDN_QKV], preferred_element_type=f32)

    for cb in range(GDN_QKV // GDN_DK):
        cols = slice(cb * GDN_DK, (cb + 1) * GDN_DK)
        acc = cw_ref[0:1, cols] * p_ref[lo - pad:hi - pad, cols]
        for kk in range(1, GDN_CONV):
            acc = acc + cw_ref[kk:kk + 1, cols] * p_ref[lo - pad + kk:hi - pad + kk, cols]
        xa = acc * jax.nn.sigmoid(acc)
        if cb < 2 * GDN_HEADS:
            xa = xa * lax.rsqrt(jnp.sum(xa * xa, axis=1, keepdims=True) + EPS)
        if cb < GDN_HEADS:
            q_ref[0, :, cols] = (xa * (GDN_DK ** -0.5)).astype(bf16)
        elif cb < 2 * GDN_HEADS:
            k_ref[0, :, slice((cb - GDN_HEADS) * GDN_DK, (cb - GDN_HEADS + 1) * GDN_DK)] = xa.astype(bf16)
        else:
            v_ref[0, :, slice((cb - 2 * GDN_HEADS) * GDN_DV, (cb - 2 * GDN_HEADS + 1) * GDN_DV)] = xa.astype(bf16)

    z_ref[0] = jnp.dot(hm, w_ref[:, GDN_QKV:4 * D], preferred_element_type=f32).astype(bf16)

    ba = jnp.dot(hm, w_ref[:, 4 * D:GDN_NPROJ], preferred_element_type=f32)
    xs = ba + dtb_ref[...]
    softplus = jnp.maximum(xs, 0.0) + jnp.log(1.0 + jnp.exp(-jnp.abs(xs)))
    g = -jnp.exp(alog_ref[...]) * softplus
    ri = lax.broadcasted_iota(jnp.int32, (GDN_TM, GDN_TM), 0)
    ci = lax.broadcasted_iota(jnp.int32, (GDN_TM, GDN_TM), 1)
    same = (ri // GDN_CHUNK) == (ci // GDN_CHUNK)
    tri_f = (same & (ci <= ri)).astype(f32)
    tri_b = (same & (ci >= ri)).astype(f32)
    gc_f = jnp.dot(tri_f, g, preferred_element_type=f32, precision=lax.Precision.HIGHEST)
    gc_b = jnp.dot(tri_b, g, preferred_element_type=f32, precision=lax.Precision.HIGHEST)
    lane = lax.broadcasted_iota(jnp.int32, (GDN_TM, 128), 1)
    gb_ref[0] = jnp.where(lane < GDN_HEADS, gc_f,
                          jnp.where(lane < 2 * GDN_HEADS, gc_b,
                                    jnp.where(lane < 4 * GDN_HEADS, jax.nn.sigmoid(ba), 0.0)))


def _gdn_proj(h, w_in, conv_w, a_log, dt_bias):
    nh = 2 * GDN_HEADS
    w = jnp.concatenate([w_in[:, :4 * D], w_in[:, 4 * D + nh:], w_in[:, 4 * D:4 * D + nh],
                         jnp.zeros((D, 128 - 2 * nh), f32)], axis=1).astype(bf16)
    alog = jnp.zeros((1, 128), f32).at[0, :nh].set(a_log.reshape(nh))
    dtb = jnp.zeros((1, 128), f32).at[0, :nh].set(dt_bias.reshape(nh))
    tile = pl.BlockSpec((1, GDN_TM, D), lambda b, j: (b, j, 0))
    rows = GDN_TM + 2 * GDN_HALO
    ncb = GRID_W // GDN_CB
    h4 = h.reshape(B, T // GRID_W, GRID_W, D)

    def cblock(j):
        return jnp.minimum(j // GDN_CT, ncb - 1)

    return pl.pallas_call(
        _gdn_proj_kernel,
        grid=(B, GDN_NT),
        in_specs=[
            pl.BlockSpec((1, GRID_ROWS, GDN_CB, D), lambda b, j: (b, 0, cblock(j), 0)),
            pl.BlockSpec((1, GDN_HALO, GDN_CB, D),
                         lambda b, j: (b, GRID_ROWS // GDN_HALO - 1, jnp.maximum(cblock(j) - 1, 0), 0)),
            pl.BlockSpec((1, GDN_HALO, GDN_CB, D), lambda b, j: (b, 0, jnp.minimum(cblock(j) + 1, ncb - 1), 0)),
            pl.BlockSpec((1, CTX // GRID_W, GRID_W, D), lambda b, j: (b, GRID_ROWS // (CTX // GRID_W), 0, 0)),
            pl.BlockSpec((D, GDN_NPROJ), lambda b, j: (0, 0), pipeline_mode=pl.Buffered(1)),
            pl.BlockSpec((GDN_CONV, GDN_QKV), lambda b, j: (0, 0)),
            pl.BlockSpec((1, 128), lambda b, j: (0, 0)),
            pl.BlockSpec((1, 128), lambda b, j: (0, 0)),
        ],
        out_specs=[tile, tile, tile, tile, pl.BlockSpec((1, GDN_TM, 128), lambda b, j: (b, j, 0))],
        out_shape=[jax.ShapeDtypeStruct((B, T, D), bf16)] * 4 + [jax.ShapeDtypeStruct((B, T, 128), f32)],
        scratch_shapes=[pltpu.VMEM((rows, GDN_QKV), f32), pltpu.VMEM((rows, D), bf16)],
        compiler_params=pltpu.CompilerParams(
            dimension_semantics=("arbitrary", "arbitrary"), vmem_limit_bytes=V7X_VMEM_LIMIT),
        name="gdn_proj",
    )(h4, h4, h4, h4, w, conv_w, alog, dtb)


GDN_PA = 2
GDN_INV_BASE = 8


def _gdn_chunk_kernel(q_ref, k_ref, v_ref, gb_ref, o_ref, s_ref, wq_sc, kd_sc, u_sc, at_sc, gl_sc):
    hp = pl.program_id(1)
    C = GDN_CHUNK
    nchain = 2 * GDN_HP
    o_ref[...] = jnp.zeros_like(o_ref)
    s_ref[...] = jnp.zeros_like(s_ref)
    ri = lax.broadcasted_iota(jnp.int32, (C, C), 0)
    ci = lax.broadcasted_iota(jnp.int32, (C, C), 1)
    eye = (ri == ci).astype(f32)
    lane = lax.broadcasted_iota(jnp.int32, (C, 128), 1)
    ones3 = jnp.ones((C, 3 * 128), bf16)
    masks = ((ci <= ri, ci < ri, C - 1), (ci >= ri, ci > ri, 0))
    diag_blk = (ri // GDN_INV_BASE) == (ci // GDN_INV_BASE)
    merge_masks = []
    n = GDN_INV_BASE
    while n < C:
        merge_masks.append(((ri // (2 * n)) == (ci // (2 * n))) & ((ri // n) != (ci // n)))
        n *= 2

    def local(m, carry):
        items = [(hl, d, e) for hl in range(GDN_HP) for d in range(2) for e in range(GDN_PA)]
        st = []
        for hl, d, e in items:
            ch = m * GDN_PA + e
            r0 = pl.multiple_of(ch * C, C)
            cols = slice(hl * GDN_DK, (hl + 1) * GDN_DK)
            q = q_ref[0, pl.ds(r0, C), cols]
            k = k_ref[0, pl.ds(r0, C), cols]
            v = v_ref[0, pl.ds(r0, C), cols].astype(f32)
            gbt = gb_ref[0, pl.ds(r0, C), :]
            lg = d * GDN_HEADS + hp * GDN_HP + hl
            gsel = jnp.where(lane == lg, gbt, 0.0)
            gc = jnp.sum(gsel, axis=1, keepdims=True)
            beta = jnp.sum(jnp.where(lane == 2 * GDN_HEADS + lg, gbt, 0.0), axis=1, keepdims=True)
            a1 = gsel.astype(bf16)
            r1 = gsel - a1.astype(f32)
            a2 = r1.astype(bf16)
            a3 = (r1 - a2.astype(f32)).astype(bf16)
            kf = k.astype(f32)
            kb = kf * beta
            st.append(dict(c=hl * 2 + d, d=d, ch=ch, q=q, k=k, v=v, gc=gc, beta=beta, kf=kf, kb=kb,
                           a3=jnp.concatenate([a1, a2, a3], axis=1)))
        for t in st:
            t["gc_row"] = _dot_nt(ones3, t.pop("a3"))
            t["kq"] = _dot_nt(jnp.concatenate([t["kb"].astype(bf16), t["q"]], axis=0), t["k"])
        for t in st:
            incl, strict, last = masks[t["d"]]
            decay = jnp.exp(jnp.where(incl, t["gc"] - t.pop("gc_row"), -jnp.inf))
            kq = t.pop("kq")
            a_mat = jnp.where(strict, kq[:C] * decay, 0.0)
            at_sc[t["c"], t["ch"]] = jnp.where(incl, kq[C:] * decay, 0.0).astype(bf16)
            t["a"] = a_mat.astype(bf16)
            x = -jnp.where(diag_blk, a_mat, 0.0)
            t["r"] = eye + x
            t["xb"] = x.astype(bf16)
        for t in st:
            xb = t.pop("xb")
            t["y"] = jnp.dot(xb, xb, preferred_element_type=f32)
        for t in st:
            yb = t.pop("y").astype(bf16)
            ry = jnp.dot(jnp.concatenate([t["r"].astype(bf16), yb], axis=0), yb, preferred_element_type=f32)
            t["r"] = t["r"] + ry[:C]
            t["y"] = ry[C:]
        for t in st:
            t["r"] = t["r"] + jnp.dot(t["r"].astype(bf16), t.pop("y").astype(bf16), preferred_element_type=f32)
        for off in merge_masks:
            for t in st:
                t["w"] = jnp.dot(jnp.where(off, t["a"], jnp.zeros_like(t["a"])), t["r"].astype(bf16),
                                 preferred_element_type=f32)
            for t in st:
                t["r"] = t["r"] - jnp.dot(t["r"].astype(bf16), t.pop("w").astype(bf16), preferred_element_type=f32)
        for t in st:
            incl, strict, last = masks[t["d"]]
            gc = t["gc"]
            gexp = jnp.exp(gc)
            rhs = jnp.concatenate([t["v"] * t["beta"], t["kb"] * gexp], axis=1)
            uw = rhs + jnp.dot((t["r"] - eye).astype(bf16), rhs.astype(bf16), preferred_element_type=f32)
            gc_last = gc[last:last + 1]
            c, ch = t["c"], t["ch"]
            u_sc[c, ch] = uw[:, :GDN_DV].astype(bf16)
            wq_sc[c, ch, 0:C, :] = uw[:, GDN_DV:].astype(bf16)
            wq_sc[c, ch, C:2 * C, :] = (t["q"].astype(f32) * gexp).astype(bf16)
            kd_sc[c, ch] = (t["kf"] * jnp.exp(gc_last - gc)).astype(bf16)
            gl_sc[c, ch] = jnp.broadcast_to(jnp.exp(gc_last), (8, 128))
        return carry

    lax.fori_loop(0, GDN_NCH // GDN_PA, local, 0)

    def recur(i, carry):
        st = []
        for hl in range(GDN_HP):
            for d in range(2):
                if d == 0:
                    ch = jnp.where(i < GDN_CCH, SEQ // C + i, i - GDN_CCH)
                else:
                    ch = jnp.where(i < GDN_CCH, SEQ // C + GDN_CCH - 1 - i, GDN_NCH - 1 - i)
                st.append(dict(c=hl * 2 + d, hl=hl, ch=ch))
        for t in st:
            c, ch = t["c"], t["ch"]
            t["s"] = s_ref[c]
            t["wq"] = jnp.dot(wq_sc[c, ch], t["s"].astype(bf16), preferred_element_type=f32)
        for t in st:
            c, ch = t["c"], t["ch"]
            wq = t.pop("wq")
            v_new = (u_sc[c, ch].astype(f32) - wq[:C]).astype(bf16)
            o = wq[C:] + jnp.dot(at_sc[c, ch], v_new, preferred_element_type=f32)
            s_ref[c] = t.pop("s") * gl_sc[c, ch][0:1, :] + lax.dot_general(
                kd_sc[c, ch], v_new, (((0,), (0,)), ((), ())), preferred_element_type=f32)
            r0 = pl.multiple_of(ch * C, C)
            o_ref[0, pl.ds(r0, C), slice(t["hl"] * GDN_DV, (t["hl"] + 1) * GDN_DV)] += o
        return carry

    lax.fori_loop(0, GDN_NCH, recur, 0)


def _gdn_chunk(q, k, v, gb):
    wide = GDN_HP * GDN_DK
    nchain = 2 * GDN_HP
    blk = pl.BlockSpec((1, T, wide), lambda b, hp: (b, 0, hp))
    return pl.pallas_call(
        _gdn_chunk_kernel,
        grid=(B, GDN_HEADS // GDN_HP),
        in_specs=[blk, blk, blk, pl.BlockSpec((1, T, 128), lambda b, hp: (b, 0, 0))],
        out_specs=blk,
        out_shape=jax.ShapeDtypeStruct((B, T, D), f32),
        scratch_shapes=[
            pltpu.VMEM((nchain, GDN_DK, GDN_DV), f32),
            pltpu.VMEM((nchain, GDN_NCH, 2 * GDN_CHUNK, GDN_DK), bf16),
            pltpu.VMEM((nchain, GDN_NCH, GDN_CHUNK, GDN_DK), bf16),
            pltpu.VMEM((nchain, GDN_NCH, GDN_CHUNK, GDN_DV), bf16),
            pltpu.VMEM((nchain, GDN_NCH, GDN_CHUNK, GDN_CHUNK), bf16),
            pltpu.VMEM((nchain, GDN_NCH, 8, 128), f32),
        ],
        compiler_params=pltpu.CompilerParams(
            dimension_semantics=("arbitrary", "arbitrary"), vmem_limit_bytes=V7X_VMEM_LIMIT),
        name="gdn_chunk",
    )(q, k, v, gb)


def _gdn_out_kernel(o_ref, z_ref, ng_ref, w_ref, y_ref):
    parts = []
    for h in range(GDN_HEADS):
        cols = slice(h * GDN_DV, (h + 1) * GDN_DV)
        oh = o_ref[0, :, cols]
        zz = z_ref[0, :, cols].astype(f32)
        yn = oh * lax.rsqrt(jnp.mean(oh * oh, axis=1, keepdims=True) + EPS) * ng_ref[...]
        parts.append((yn * (zz * jax.nn.sigmoid(zz))).astype(bf16))
    y_ref[0] = jnp.dot(jnp.concatenate(parts, axis=1), w_ref[...], preferred_element_type=f32).astype(bf16)


def _gdn_out(o, z, norm_g, w_out):
    tile = pl.BlockSpec((1, GDN_TM, D), lambda b, j: (b, j, 0))
    return pl.pallas_call(
        _gdn_out_kernel,
        grid=(B, GDN_NT),
        in_specs=[tile, tile, pl.BlockSpec((1, GDN_DV), lambda b, j: (0, 0)),
                  pl.BlockSpec((D, D), lambda b, j: (0, 0))],
        out_specs=tile,
        out_shape=jax.ShapeDtypeStruct((B, T, D), bf16),
        compiler_params=pltpu.CompilerParams(
            dimension_semantics=("arbitrary", "arbitrary"), vmem_limit_bytes=V7X_VMEM_LIMIT),
        name="gdn_out",
    )(o, z, norm_g, w_out)


def _token_order(y):
    lat = y[:, :SEQ].reshape(B, GRID_W, GRID_ROWS, D).transpose(0, 2, 1, 3).reshape(B, SEQ, D)
    return jnp.concatenate([lat, y[:, SEQ:]], axis=1)


def _gdn_mixer(h, w_in, conv_w, a_log, dt_bias, norm_g, w_out):
    q, k, v, z, gb = _gdn_proj(h, w_in, conv_w, a_log, dt_bias)
    o = _gdn_chunk(q, k, v, gb)
    return _token_order(_gdn_out(o, z, norm_g[None], w_out.astype(bf16)))


def kernel(x, c, ctx, c_ctx, mod_w, mod_b, norm_g, ffn_w13, ffn_w2, s5_a_re, s5_a_im, s5_log_dt,
           s5_b_re, s5_b_im, s5_c_re, s5_c_im, s5_d, s5_glu_w, s5_glu_b, gdn_w_in, gdn_conv_w,
           gdn_a_log, gdn_dt_bias, gdn_norm_g, gdn_w_out, final_g):
    tok = jnp.concatenate([x, ctx], axis=1)
    mods = _modulation(c, c_ctx, mod_w, mod_b).reshape(DEPTH, MOD_ROWS, N_MOD, D)
    w13 = ffn_w13.astype(bf16)
    w2 = ffn_w2.astype(bf16)
    for i in range(DEPTH):
        m = mods[i]
        tok, h = _ffn(tok, m[:, 0:3], norm_g[i, 0][None], w13, w2, i, 0,
                      next_norm=(m[:, 3:6], norm_g[i, 1][None]))
        j = i // 2
        if i % 2 == 0:
            y = _s5_unflatten(_s5_core(_s5_flatten(h), s5_a_re[j], s5_a_im[j], s5_log_dt[j], s5_b_re[j],
                                       s5_b_im[j], s5_c_re[j], s5_c_im[j], s5_d[j]))
            tok = _glu_residual(tok, y, m[:, 3:6], s5_glu_w[j].astype(bf16), s5_glu_b[j][None])
            tok = _ffn(tok, m[:, 6:9], norm_g[i, 2][None], w13, w2, i, 1)
        else:
            y = _gdn_mixer(h, gdn_w_in[j], gdn_conv_w[j], gdn_a_log[j], gdn_dt_bias[j], gdn_norm_g[j],
                           gdn_w_out[j])
            tok = _ffn(tok, m[:, 6:9], norm_g[i, 2][None], w13, w2, i, 1, mixer_out=y, m3_mixer=m[:, 3:6])
    return _final_norm(tok, final_g[None])
```

```python
import functools

import jax
import jax.numpy as jnp
from jax import lax
from jax.experimental import pallas as pl
from jax.experimental.pallas import tpu as pltpu

D = 1024
B = 4
SEQ = 4096
CTX = 256
T = SEQ + CTX
DEPTH = 4
GRID_W = 64
N_MOD = 9
D_FF = 2816
S5_GROUP = 16
S5_GROUPS = D // S5_GROUP
S5_STATE = 64
GDN_HEADS = 8
GDN_DK = D // GDN_HEADS
GDN_DV = D // GDN_HEADS
GDN_CONV = 5
GDN_CHUNK = 128
EPS = 1e-6

MOD_ROWS = 8
V7X_VMEM_LIMIT = 56 * 1024 * 1024

f32 = jnp.float32
bf16 = jnp.bfloat16


MOD_TN = 2304


def _mod_kernel(c_ref, w_ref, b_ref, o_ref):
    c = c_ref[...]
    s = c * jax.nn.sigmoid(c)
    o_ref[0] = jnp.dot(s, w_ref[0], preferred_element_type=f32,
                       precision=lax.Precision.HIGHEST) + b_ref[0]


def _modulation(c, c_ctx, mod_w, mod_b):
    cc = jnp.zeros((MOD_ROWS, D), f32).at[:B].set(c).at[B].set(c_ctx)
    n = N_MOD * D
    return pl.pallas_call(
        _mod_kernel,
        grid=(DEPTH, n // MOD_TN),
        in_specs=[
            pl.BlockSpec((MOD_ROWS, D), lambda i, j: (0, 0)),
            pl.BlockSpec((1, D, MOD_TN), lambda i, j: (i, 0, j)),
            pl.BlockSpec((1, 1, MOD_TN), lambda i, j: (i, 0, j)),
        ],
        out_specs=pl.BlockSpec((1, MOD_ROWS, MOD_TN), lambda i, j: (i, 0, j)),
        out_shape=jax.ShapeDtypeStruct((DEPTH, MOD_ROWS, n), f32),
        compiler_params=pltpu.CompilerParams(
            dimension_semantics=("arbitrary", "arbitrary"), vmem_limit_bytes=V7X_VMEM_LIMIT),
        name="modulation",
    )(cc, mod_w, mod_b.reshape(DEPTH, 1, n))


def _row_modulation(m_ref, b, t, tm):
    rows = t * tm + lax.broadcasted_iota(jnp.int32, (tm, 1), 0)
    is_ctx = rows >= SEQ
    mx = m_ref[b]
    mc = m_ref[B]
    return tuple(jnp.where(is_ctx, mc[j:j + 1], mx[j:j + 1]) for j in range(3))


def _ada_norm(x, g, shift, scale):
    ms = jnp.mean(x * x, axis=-1, keepdims=True)
    return x * lax.rsqrt(ms + EPS) * g * (1.0 + scale) + shift


FFN_TM = 1088
FFN_CHUNK = 256


def _ffn_kernel(*refs, with_mixer_out, with_next_norm):
    refs = list(refs)
    if with_mixer_out:
        mm_ref, y_ref = refs[:2]
        refs = refs[2:]
    m_ref, g_ref, x_ref, w13_ref, w2_ref = refs[:5]
    refs = refs[5:]
    if with_next_norm:
        mn_ref, gn_ref, o_ref, h_ref = refs
    else:
        (o_ref,) = refs
    b = pl.program_id(0)
    t = pl.program_id(1)
    x = x_ref[0]
    if with_mixer_out:
        _, _, gate_mix = _row_modulation(mm_ref, b, t, FFN_TM)
        x = x + gate_mix * y_ref[0].astype(f32)
    shift, scale, gate = _row_modulation(m_ref, b, t, FFN_TM)
    h = _ada_norm(x, g_ref[...], shift, scale).astype(bf16)
    acc = jnp.zeros((FFN_TM, D), f32)
    for j in range(D_FF // FFN_CHUNK):
        lo = j * FFN_CHUNK
        a = jnp.dot(h, w13_ref[:, lo:lo + FFN_CHUNK], preferred_element_type=f32)
        u = jnp.dot(h, w13_ref[:, D_FF + lo:D_FF + lo + FFN_CHUNK], preferred_element_type=f32)
        hid = (a * jax.nn.sigmoid(a) * u).astype(bf16)
        acc = acc + jnp.dot(hid, w2_ref[lo:lo + FFN_CHUNK, :], preferred_element_type=f32)
    out = x + 0.5 * gate * acc
    o_ref[0] = out
    if with_next_norm:
        shift_n, scale_n, _ = _row_modulation(mn_ref, b, t, FFN_TM)
        h_ref[0] = _ada_norm(out, gn_ref[...], shift_n, scale_n).astype(bf16)


def _ffn(tok, m3, g, w13, w2, layer, which, mixer_out=None, m3_mixer=None, next_norm=None):
    mod = pl.BlockSpec((MOD_ROWS, 3, D), lambda b, t: (0, 0, 0))
    gain = pl.BlockSpec((1, D), lambda b, t: (0, 0))
    tile = pl.BlockSpec((1, FFN_TM, D), lambda b, t: (b, t, 0))
    in_specs = [
        mod, gain, tile,
        pl.BlockSpec((None, None, D, 2 * D_FF), lambda b, t: (layer, which, 0, 0), pipeline_mode=pl.Buffered(1)),
        pl.BlockSpec((None, None, D_FF, D), lambda b, t: (layer, which, 0, 0), pipeline_mode=pl.Buffered(1)),
    ]
    args = (m3, g, tok, w13, w2)
    out_specs = tile
    out_shape = jax.ShapeDtypeStruct((B, T, D), f32)
    if mixer_out is not None:
        in_specs = [mod, tile] + in_specs
        args = (m3_mixer, mixer_out) + args
    if next_norm is not None:
        in_specs = in_specs + [mod, gain]
        args = args + tuple(next_norm)
        out_specs = [tile, tile]
        out_shape = [out_shape, jax.ShapeDtypeStruct((B, T, D), bf16)]
    return pl.pallas_call(
        functools.partial(_ffn_kernel, with_mixer_out=mixer_out is not None,
                          with_next_norm=next_norm is not None),
        grid=(B, T // FFN_TM),
        in_specs=in_specs,
        out_specs=out_specs,
        out_shape=out_shape,
        compiler_params=pltpu.CompilerParams(
            dimension_semantics=("arbitrary", "arbitrary"), vmem_limit_bytes=V7X_VMEM_LIMIT),
        name="ffn",
    )(*args)


FINAL_TM = 512


def _final_kernel(g_ref, x_ref, o_ref):
    x = x_ref[0]
    ms = jnp.mean(x * x, axis=-1, keepdims=True)
    o_ref[0] = x * lax.rsqrt(ms + EPS) * g_ref[...]


def _final_norm(tok, g):
    return pl.pallas_call(
        _final_kernel,
        grid=(B, SEQ // FINAL_TM),
        in_specs=[
            pl.BlockSpec((1, D), lambda b, t: (0, 0)),
            pl.BlockSpec((1, FINAL_TM, D), lambda b, t: (b, t, 0)),
        ],
        out_specs=pl.BlockSpec((1, FINAL_TM, D), lambda b, t: (b, t, 0)),
        out_shape=jax.ShapeDtypeStruct((B, SEQ, D), f32),
        compiler_params=pltpu.CompilerParams(dimension_semantics=("arbitrary", "arbitrary")),
        name="final_norm",
    )(g, tok)


S5_L = 64
S5_NCH = T // S5_L
S5_CCH = CTX // S5_L
S5_ROWS = S5_NCH * B
S5_W = S5_L * S5_GROUP
S5_BASE = 8


def _cmul(ar, ai, br, bi):
    return ar * br - ai * bi, ar * bi + ai * br


def _dot_nt(a, b):
    return lax.dot_general(a, b, (((1,), (1,)), ((), ())), preferred_element_type=f32)


def _s5_kernel(u_ref, are_ref, aim_ref, ldt_ref, btre_ref, btim_ref, cre_ref, cim_ref, dt_ref, y_ref,
               m_ref, s_ref, hf_ref, hb_ref):
    P = S5_STATE
    lane = lax.broadcasted_iota(jnp.int32, (1, 2 * P), 1)
    is_f = lane < P
    mf = is_f.astype(f32)
    mb = 1.0 - mf

    a_re = are_ref[0]
    a_im = aim_ref[0]
    dt = jnp.exp(ldt_ref[0])
    mag = jnp.exp(dt * a_re)
    ang = dt * a_im
    l_re = mag * jnp.cos(ang)
    l_im = mag * jnp.sin(ang)
    den = a_re * a_re + a_im * a_im
    n_re = l_re - 1.0
    f_re = (n_re * a_re + l_im * a_im) / den
    f_im = (l_im * a_re - n_re * a_im) / den

    c_re = cre_ref[0]
    c_im = cim_ref[0]
    a_r, a_i = _cmul(f_re, f_im, btre_ref[0], btim_ref[0])
    r_r, r_i = _cmul(l_re, l_im, c_re, c_im)
    p_re, p_im = l_re, l_im
    for _ in range(6):
        fm_re = jnp.where(is_f, p_re, 1.0)
        fm_im = jnp.where(is_f, p_im, 0.0)
        bm_re = jnp.where(is_f, 1.0, p_re)
        bm_im = jnp.where(is_f, 0.0, p_im)
        at_r, at_i = _cmul(fm_re, fm_im, a_r, a_i)
        ab_r, ab_i = _cmul(bm_re, bm_im, a_r, a_i)
        rt_r, rt_i = _cmul(bm_re, bm_im, r_r, r_i)
        rb_r, rb_i = _cmul(fm_re, fm_im, r_r, r_i)
        a_r = jnp.concatenate([at_r, ab_r], axis=0)
        a_i = jnp.concatenate([at_i, ab_i], axis=0)
        r_r = jnp.concatenate([rt_r, rb_r], axis=0)
        r_i = jnp.concatenate([rt_i, rb_i], axis=0)
        p_re, p_im = _cmul(p_re, p_im, p_re, p_im)
    acat = jnp.concatenate([a_r, a_i], axis=1).astype(bf16)
    rcat = jnp.concatenate([r_r, -r_i], axis=1)
    mf2 = jnp.concatenate([mf, mf], axis=1)
    rtf = (rcat * mf2).astype(bf16)
    rtb = (rcat * (1.0 - mf2)).astype(bf16)
    rcat = rcat.astype(bf16)

    nb = S5_BASE
    ccat = jnp.concatenate([c_re, -c_im], axis=1)
    ctf = jnp.concatenate([(ccat * mf2).astype(bf16)] * nb, axis=0)
    ctb = jnp.concatenate([(ccat * (1.0 - mf2)).astype(bf16)] * nb, axis=0)
    bw = nb * S5_GROUP
    kf_all = _dot_nt(acat[S5_W - bw:], ctf)
    kb_all = _dot_nt(acat[:bw], ctb)
    lane_t0 = lax.broadcasted_iota(jnp.int32, (S5_GROUP, bw), 1) // S5_GROUP
    lane_i = lax.broadcasted_iota(jnp.int32, (S5_GROUP, bw), 1)
    row_i = lax.broadcasted_iota(jnp.int32, (S5_GROUP, bw), 0)
    dtile = dt_ref[0]
    blocks = []
    for s0 in range(nb):
        acc = jnp.where(lane_i == s0 * S5_GROUP + row_i, dtile, 0.0)
        for lag in range(nb - s0):
            acc = acc + jnp.where(lane_t0 == s0 + lag, kf_all[(nb - 1 - lag) * S5_GROUP:(nb - lag) * S5_GROUP], 0.0)
        for lag in range(s0 + 1):
            acc = acc + jnp.where(lane_t0 == s0 - lag, kb_all[lag * S5_GROUP:(lag + 1) * S5_GROUP], 0.0)
        blocks.append(acc)
    blk0 = jnp.concatenate(blocks, axis=0).astype(bf16)
    for i in range(S5_L // nb):
        m_ref[i * bw:(i + 1) * bw, i * bw:(i + 1) * bw] = blk0
    n = nb
    while n < S5_L:
        w = n * S5_GROUP
        xn = _dot_nt(acat[S5_W - w:], rtf[:w]).astype(bf16)
        yn = _dot_nt(acat[:w], rtb[S5_W - w:]).astype(bf16)
        for j in range(S5_L // (2 * n)):
            o = 2 * j * w
            m_ref[o:o + w, o + w:o + 2 * w] = xn
            m_ref[o + w:o + 2 * w, o:o + w] = yn
        n *= 2

    u = u_ref[0]
    y = jnp.dot(u, m_ref[...], preferred_element_type=f32)
    s_ref[...] = jnp.dot(u, acat, preferred_element_type=f32)

    h_re = jnp.zeros((B, 2 * P), f32)
    h_im = jnp.zeros((B, 2 * P), f32)
    for j in range(S5_NCH):
        lat = S5_NCH - S5_CCH
        kf = lat + j if j < S5_CCH else j - S5_CCH
        kb = lat + S5_CCH - 1 - j if j < S5_CCH else S5_NCH - 1 - j
        hf_ref[kf * B:(kf + 1) * B, :] = jnp.concatenate([h_re, h_im], axis=1)
        hb_ref[kb * B:(kb + 1) * B, :] = jnp.concatenate([h_re, h_im], axis=1)
        s_re = jnp.where(is_f, s_ref[kf * B:(kf + 1) * B, 0:2 * P], s_ref[kb * B:(kb + 1) * B, 0:2 * P])
        s_im = jnp.where(is_f, s_ref[kf * B:(kf + 1) * B, 2 * P:4 * P], s_ref[kb * B:(kb + 1) * B, 2 * P:4 * P])
        h_re, h_im = (p_re * h_re - p_im * h_im + s_re, p_re * h_im + p_im * h_re + s_im)
    hcat = jnp.where(mf2 > 0.5, hf_ref[...], hb_ref[...]).astype(bf16)
    y_ref[0] = (y + _dot_nt(hcat, rcat)).astype(bf16)


def _s5_core(uflat, a_re, a_im, log_dt, b_re, b_im, c_re, c_im, d):
    G = S5_GROUPS

    def pack(t):
        return jnp.concatenate([t[0], t[1]], axis=-1)

    are = pack(a_re)[:, None, :]
    aim = pack(a_im)[:, None, :]
    ldt = pack(jnp.broadcast_to(log_dt[:, :, None], (2, G, S5_STATE)))[:, None, :]
    btre = pack(jnp.swapaxes(b_re, -1, -2))
    btim = pack(jnp.swapaxes(b_im, -1, -2))
    cre = pack(c_re)
    cim = pack(c_im)
    dtile = jnp.tile(d.reshape(G, 1, S5_GROUP), (1, 1, S5_BASE))
    vec = pl.BlockSpec((1, 1, 2 * S5_STATE), lambda g: (g, 0, 0))
    mat = pl.BlockSpec((1, S5_GROUP, 2 * S5_STATE), lambda g: (g, 0, 0))
    big = pl.BlockSpec((1, S5_ROWS, S5_W), lambda g: (g, 0, 0))
    return pl.pallas_call(
        _s5_kernel,
        grid=(G,),
        in_specs=[big, vec, vec, vec, mat, mat, mat, mat, vec],
        out_specs=big,
        out_shape=jax.ShapeDtypeStruct((G, S5_ROWS, S5_W), bf16),
        scratch_shapes=[
            pltpu.VMEM((S5_W, S5_W), bf16),
            pltpu.VMEM((S5_ROWS, 4 * S5_STATE), f32),
            pltpu.VMEM((S5_ROWS, 4 * S5_STATE), f32),
            pltpu.VMEM((S5_ROWS, 4 * S5_STATE), f32),
        ],
        compiler_params=pltpu.CompilerParams(
            dimension_semantics=("arbitrary",), vmem_limit_bytes=V7X_VMEM_LIMIT),
        name="s5_core",
    )(uflat, are, aim, ldt, btre, btim, cre, cim, dtile)


SEG_TM = 1088
SEG_UNROLL = 8


def _seg_transpose_kernel(x_ref, o_ref):
    a = lax.broadcasted_iota(jnp.int32, (8, D), 0)
    b = (lax.broadcasted_iota(jnp.int32, (8, D), 1) // S5_GROUP) % 8

    def body(i, carry):
        rows = [pl.multiple_of((i * SEG_UNROLL + u) * 8, 8) for u in range(SEG_UNROLL)]
        xs = [x_ref[0, pl.ds(r, 8), :].astype(f32) for r in rows]
        for j in range(3):
            d = 1 << j
            same = ((a >> j) & 1) == ((b >> j) & 1)
            low = ((a >> j) & 1) == 0
            ups = [pltpu.roll(pltpu.roll(x, 8 - d, 0), S5_GROUP * d, 1) for x in xs]
            dns = [pltpu.roll(pltpu.roll(x, d, 0), D - S5_GROUP * d, 1) for x in xs]
            xs = [jnp.where(same, x, jnp.where(low, up, dn)) for x, up, dn in zip(xs, ups, dns)]
        for r, x in zip(rows, xs):
            o_ref[0, pl.ds(r, 8), :] = x.astype(o_ref.dtype)
        return carry

    lax.fori_loop(0, SEG_TM // (8 * SEG_UNROLL), body, 0)


def _seg_transpose(x):
    tile = pl.BlockSpec((1, SEG_TM, D), lambda b, t: (b, t, 0))
    return pl.pallas_call(
        _seg_transpose_kernel,
        grid=(B, T // SEG_TM),
        in_specs=[tile],
        out_specs=tile,
        out_shape=jax.ShapeDtypeStruct((B, T, D), x.dtype),
        compiler_params=pltpu.CompilerParams(dimension_semantics=("arbitrary", "arbitrary")),
        name="seg_transpose",
    )(x)


def _s5_flatten(h):
    hp = _seg_transpose(h).reshape(B, S5_NCH, S5_L // 8, 8, S5_GROUPS // 8, 128)
    return hp.transpose(4, 3, 1, 0, 2, 5).reshape(S5_GROUPS, S5_ROWS, S5_W)


def _s5_unflatten(y):
    yp = y.reshape(S5_GROUPS // 8, 8, S5_NCH, B, S5_L // 8, 128)
    return _seg_transpose(yp.transpose(3, 2, 4, 1, 0, 5).reshape(B, T, D))


GLU_TM = 544


def _glu_kernel(m_ref, y_ref, x_ref, w_ref, b_ref, o_ref):
    b = pl.program_id(0)
    t = pl.program_id(1)
    _, _, gate = _row_modulation(m_ref, b, t, GLU_TM)
    z = jax.nn.gelu(y_ref[0].astype(f32)).astype(bf16)
    p = jnp.dot(z, w_ref[...], preferred_element_type=f32) + b_ref[...]
    o_ref[0] = x_ref[0] + gate * (p[:, :D] * jax.nn.sigmoid(p[:, D:]))


def _glu_residual(tok, y, m3, glu_w, glu_b):
    tile = pl.BlockSpec((1, GLU_TM, D), lambda b, t: (b, t, 0))
    return pl.pallas_call(
        _glu_kernel,
        grid=(B, T // GLU_TM),
        in_specs=[
            pl.BlockSpec((MOD_ROWS, 3, D), lambda b, t: (0, 0, 0)),
            tile, tile,
            pl.BlockSpec((D, 2 * D), lambda b, t: (0, 0)),
            pl.BlockSpec((1, 2 * D), lambda b, t: (0, 0)),
        ],
        out_specs=tile,
        out_shape=jax.ShapeDtypeStruct((B, T, D), f32),
        compiler_params=pltpu.CompilerParams(
            dimension_semantics=("arbitrary", "arbitrary"), vmem_limit_bytes=V7X_VMEM_LIMIT),
        name="glu_residual",
    )(m3, y, tok, glu_w, glu_b)


GDN_TM = 256
GDN_NT = T // GDN_TM
GRID_ROWS = SEQ // GRID_W
GDN_WT = GDN_TM // GRID_ROWS
GDN_CB = 16
GDN_CT = GDN_CB // GDN_WT
assert CTX == GDN_TM and SEQ % GDN_TM == 0 and GRID_W % GDN_CB == 0
GDN_HALO = 16
GDN_QKV = 3 * D
GDN_NPROJ = 4 * D + 128
GDN_NCH = T // GDN_CHUNK
GDN_CCH = CTX // GDN_CHUNK
GDN_HP = 2


def _gdn_proj_kernel(main_ref, prev_ref, next_ref, ctx_ref, w_ref, cw_ref, alog_ref, dtb_ref,
                     q_ref, k_ref, v_ref, z_ref, gb_ref, p_ref, hbuf):
    j = pl.program_id(1)
    pad = GDN_CONV // 2
    lo = GDN_HALO
    hi = GDN_HALO + GDN_TM
    top = GRID_ROWS - GDN_HALO

    for jm in range(GDN_CT):
        @pl.when((j < GDN_NT - 1) & (j % GDN_CT == jm))
        def _(jm=jm):
            w0 = jm * GDN_WT
            for r in range(GDN_WT):
                hbuf[lo + r * GRID_ROWS:lo + (r + 1) * GRID_ROWS, :] = main_ref[0, :, w0 + r, :]
            hbuf[0:lo, :] = main_ref[0, top:, w0 - 1, :] if jm > 0 else prev_ref[0, :, GDN_CB - 1, :]
            hbuf[hi:hi + GDN_HALO, :] = (main_ref[0, 0:GDN_HALO, w0 + GDN_WT, :] if jm < GDN_CT - 1
                                         else next_ref[0, :, 0, :])

    @pl.when(j == GDN_NT - 1)
    def _():
        hbuf[lo:hi, :] = ctx_ref[0].reshape(GDN_TM, D)

    @pl.when((j == 0) | (j == GDN_NT - 1))
    def _():
        hbuf[0:lo, :] = jnp.zeros((GDN_HALO, D), bf16)

    @pl.when(j >= GDN_NT - 2)
    def _():
        hbuf[hi:hi + GDN_HALO, :] = jnp.zeros((GDN_HALO, D), bf16)

    hm = hbuf[lo:hi, :]
    p_ref[...] = jnp.dot(hbuf[...], w_ref[:, 0:GDN_QKV], preferred_element_type=f32)

    for cb in range(GDN_QKV // GDN_DK):
        cols = slice(cb * GDN_DK, (cb + 1) * GDN_DK)
        acc = cw_ref[0:1, cols] * p_ref[lo - pad:hi - pad, cols]
        for kk in range(1, GDN_CONV):
            acc = acc + cw_ref[kk:kk + 1, cols] * p_ref[lo - pad + kk:hi - pad + kk, cols]
        xa = acc * jax.nn.sigmoid(acc)
        if cb < 2 * GDN_HEADS:
            xa = xa * lax.rsqrt(jnp.sum(xa * xa, axis=1, keepdims=True) + EPS)
        if cb < GDN_HEADS:
            q_ref[0, :, cols] = (xa * (GDN_DK ** -0.5)).astype(bf16)
        elif cb < 2 * GDN_HEADS:
            k_ref[0, :, slice((cb - GDN_HEADS) * GDN_DK, (cb - GDN_HEADS + 1) * GDN_DK)] = xa.astype(bf16)
        else:
            v_ref[0, :, slice((cb - 2 * GDN_HEADS) * GDN_DV, (cb - 2 * GDN_HEADS + 1) * GDN_DV)] = xa.astype(bf16)

    z_ref[0] = jnp.dot(hm, w_ref[:, GDN_QKV:4 * D], preferred_element_type=f32).astype(bf16)

    ba = jnp.dot(hm, w_ref[:, 4 * D:GDN_NPROJ], preferred_element_type=f32)
    xs = ba + dtb_ref[...]
    softplus = jnp.maximum(xs, 0.0) + jnp.log(1.0 + jnp.exp(-jnp.abs(xs)))
    g = -jnp.exp(alog_ref[...]) * softplus
    ri = lax.broadcasted_iota(jnp.int32, (GDN_TM, GDN_TM), 0)
    ci = lax.broadcasted_iota(jnp.int32, (GDN_TM, GDN_TM), 1)
    same = (ri // GDN_CHUNK) == (ci // GDN_CHUNK)
    tri_f = (same & (ci <= ri)).astype(f32)
    tri_b = (same & (ci >= ri)).astype(f32)
    gc_f = jnp.dot(tri_f, g, preferred_element_type=f32, precision=lax.Precision.HIGHEST)
    gc_b = jnp.dot(tri_b, g, preferred_element_type=f32, precision=lax.Precision.HIGHEST)
    lane = lax.broadcasted_iota(jnp.int32, (GDN_TM, 128), 1)
    gb_ref[0] = jnp.where(lane < GDN_HEADS, gc_f,
                          jnp.where(lane < 2 * GDN_HEADS, gc_b,
                                    jnp.where(lane < 4 * GDN_HEADS, jax.nn.sigmoid(ba), 0.0)))


def _gdn_proj(h, w_in, conv_w, a_log, dt_bias):
    nh = 2 * GDN_HEADS
    w = jnp.concatenate([w_in[:, :4 * D], w_in[:, 4 * D + nh:], w_in[:, 4 * D:4 * D + nh],
                         jnp.zeros((D, 128 - 2 * nh), f32)], axis=1).astype(bf16)
    alog = jnp.zeros((1, 128), f32).at[0, :nh].set(a_log.reshape(nh))
    dtb = jnp.zeros((1, 128), f32).at[0, :nh].set(dt_bias.reshape(nh))
    tile = pl.BlockSpec((1, GDN_TM, D), lambda b, j: (b, j, 0))
    rows = GDN_TM + 2 * GDN_HALO
    ncb = GRID_W // GDN_CB
    h4 = h.reshape(B, T // GRID_W, GRID_W, D)

    def cblock(j):
        return jnp.minimum(j // GDN_CT, ncb - 1)

    return pl.pallas_call(
        _gdn_proj_kernel,
        grid=(B, GDN_NT),
        in_specs=[
            pl.BlockSpec((1, GRID_ROWS, GDN_CB, D), lambda b, j: (b, 0, cblock(j), 0)),
            pl.BlockSpec((1, GDN_HALO, GDN_CB, D),
                         lambda b, j: (b, GRID_ROWS // GDN_HALO - 1, jnp.maximum(cblock(j) - 1, 0), 0)),
            pl.BlockSpec((1, GDN_HALO, GDN_CB, D), lambda b, j: (b, 0, jnp.minimum(cblock(j) + 1, ncb - 1), 0)),
            pl.BlockSpec((1, CTX // GRID_W, GRID_W, D), lambda b, j: (b, GRID_ROWS // (CTX // GRID_W), 0, 0)),
            pl.BlockSpec((D, GDN_NPROJ), lambda b, j: (0, 0), pipeline_mode=pl.Buffered(1)),
            pl.BlockSpec((GDN_CONV, GDN_QKV), lambda b, j: (0, 0)),
            pl.BlockSpec((1, 128), lambda b, j: (0, 0)),
            pl.BlockSpec((1, 128), lambda b, j: (0, 0)),
        ],
        out_specs=[tile, tile, tile, tile, pl.BlockSpec((1, GDN_TM, 128), lambda b, j: (b, j, 0))],
        out_shape=[jax.ShapeDtypeStruct((B, T, D), bf16)] * 4 + [jax.ShapeDtypeStruct((B, T, 128), f32)],
        scratch_shapes=[pltpu.VMEM((rows, GDN_QKV), f32), pltpu.VMEM((rows, D), bf16)],
        compiler_params=pltpu.CompilerParams(
            dimension_semantics=("arbitrary", "arbitrary"), vmem_limit_bytes=V7X_VMEM_LIMIT),
        name="gdn_proj",
    )(h4, h4, h4, h4, w, conv_w, alog, dtb)


GDN_PA = 2
GDN_INV_BASE = 8


def _gdn_chunk_kernel(q_ref, k_ref, v_ref, gb_ref, o_ref, s_ref, wq_sc, kd_sc, u_sc, at_sc, gl_sc):
    hp = pl.program_id(1)
    C = GDN_CHUNK
    nchain = 2 * GDN_HP
    o_ref[...] = jnp.zeros_like(o_ref)
    s_ref[...] = jnp.zeros_like(s_ref)
    ri = lax.broadcasted_iota(jnp.int32, (C, C), 0)
    ci = lax.broadcasted_iota(jnp.int32, (C, C), 1)
    eye = (ri == ci).astype(f32)
    lane = lax.broadcasted_iota(jnp.int32, (C, 128), 1)
    ones3 = jnp.ones((C, 3 * 128), bf16)
    masks = ((ci <= ri, ci < ri, C - 1), (ci >= ri, ci > ri, 0))
    diag_blk = (ri // GDN_INV_BASE) == (ci // GDN_INV_BASE)
    merge_masks = []
    n = GDN_INV_BASE
    while n < C:
        merge_masks.append(((ri // (2 * n)) == (ci // (2 * n))) & ((ri // n) != (ci // n)))
        n *= 2

    def local(m, carry):
        items = [(hl, d, e) for hl in range(GDN_HP) for d in range(2) for e in range(GDN_PA)]
        st = []
        for hl, d, e in items:
            ch = m * GDN_PA + e
            r0 = pl.multiple_of(ch * C, C)
            cols = slice(hl * GDN_DK, (hl + 1) * GDN_DK)
            q = q_ref[0, pl.ds(r0, C), cols]
            k = k_ref[0, pl.ds(r0, C), cols]
            v = v_ref[0, pl.ds(r0, C), cols].astype(f32)
            gbt = gb_ref[0, pl.ds(r0, C), :]
            lg = d * GDN_HEADS + hp * GDN_HP + hl
            gsel = jnp.where(lane == lg, gbt, 0.0)
            gc = jnp.sum(gsel, axis=1, keepdims=True)
            beta = jnp.sum(jnp.where(lane == 2 * GDN_HEADS + lg, gbt, 0.0), axis=1, keepdims=True)
            a1 = gsel.astype(bf16)
            r1 = gsel - a1.astype(f32)
            a2 = r1.astype(bf16)
            a3 = (r1 - a2.astype(f32)).astype(bf16)
            kf = k.astype(f32)
            kb = kf * beta
            st.append(dict(c=hl * 2 + d, d=d, ch=ch, q=q, k=k, v=v, gc=gc, beta=beta, kf=kf, kb=kb,
                           a3=jnp.concatenate([a1, a2, a3], axis=1)))
        for t in st:
            t["gc_row"] = _dot_nt(ones3, t.pop("a3"))
            t["kq"] = _dot_nt(jnp.concatenate([t["kb"].astype(bf16), t["q"]], axis=0), t["k"])
        for t in st:
            incl, strict, last = masks[t["d"]]
            decay = jnp.exp(jnp.where(incl, t["gc"] - t.pop("gc_row"), -jnp.inf))
            kq = t.pop("kq")
            a_mat = jnp.where(strict, kq[:C] * decay, 0.0)
            at_sc[t["c"], t["ch"]] = jnp.where(incl, kq[C:] * decay, 0.0).astype(bf16)
            t["a"] = a_mat.astype(bf16)
            x = -jnp.where(diag_blk, a_mat, 0.0)
            t["r"] = eye + x
            t["xb"] = x.astype(bf16)
        for t in st:
            xb = t.pop("xb")
            t["y"] = jnp.dot(xb, xb, preferred_element_type=f32)
        for t in st:
            yb = t.pop("y").astype(bf16)
            ry = jnp.dot(jnp.concatenate([t["r"].astype(bf16), yb], axis=0), yb, preferred_element_type=f32)
            t["r"] = t["r"] + ry[:C]
            t["y"] = ry[C:]
        for t in st:
            t["r"] = t["r"] + jnp.dot(t["r"].astype(bf16), t.pop("y").astype(bf16), preferred_element_type=f32)
        for off in merge_masks:
            for t in st:
                t["w"] = jnp.dot(jnp.where(off, t["a"], jnp.zeros_like(t["a"])), t["r"].astype(bf16),
                                 preferred_element_type=f32)
            for t in st:
                t["r"] = t["r"] - jnp.dot(t["r"].astype(bf16), t.pop("w").astype(bf16), preferred_element_type=f32)
        for t in st:
            incl, strict, last = masks[t["d"]]
            gc = t["gc"]
            gexp = jnp.exp(gc)
            rhs = jnp.concatenate([t["v"] * t["beta"], t["kb"] * gexp], axis=1)
            uw = rhs + jnp.dot((t["r"] - eye).astype(bf16), rhs.astype(bf16), preferred_element_type=f32)
            gc_last = gc[last:last + 1]
            c, ch = t["c"], t["ch"]
            u_sc[c, ch] = uw[:, :GDN_DV].astype(bf16)
            wq_sc[c, ch, 0:C, :] = uw[:, GDN_DV:].astype(bf16)
            wq_sc[c, ch, C:2 * C, :] = (t["q"].astype(f32) * gexp).astype(bf16)
            kd_sc[c, ch] = (t["kf"] * jnp.exp(gc_last - gc)).astype(bf16)
            gl_sc[c, ch] = jnp.broadcast_to(jnp.exp(gc_last), (8, 128))
        return carry

    lax.fori_loop(0, GDN_NCH // GDN_PA, local, 0)

    def recur(i, carry):
        st = []
        for hl in range(GDN_HP):
            for d in range(2):
                if d == 0:
                    ch = jnp.where(i < GDN_CCH, SEQ // C + i, i - GDN_CCH)
                else:
                    ch = jnp.where(i < GDN_CCH, SEQ // C + GDN_CCH - 1 - i, GDN_NCH - 1 - i)
                st.append(dict(c=hl * 2 + d, hl=hl, ch=ch))
        for t in st:
            c, ch = t["c"], t["ch"]
            t["s"] = s_ref[c]
            t["wq"] = jnp.dot(wq_sc[c, ch], t["s"].astype(bf16), preferred_element_type=f32)
        for t in st:
            c, ch = t["c"], t["ch"]
            wq = t.pop("wq")
            v_new = (u_sc[c, ch].astype(f32) - wq[:C]).astype(bf16)
            o = wq[C:] + jnp.dot(at_sc[c, ch], v_new, preferred_element_type=f32)
            s_ref[c] = t.pop("s") * gl_sc[c, ch][0:1, :] + lax.dot_general(
                kd_sc[c, ch], v_new, (((0,), (0,)), ((), ())), preferred_element_type=f32)
            r0 = pl.multiple_of(ch * C, C)
            o_ref[0, pl.ds(r0, C), slice(t["hl"] * GDN_DV, (t["hl"] + 1) * GDN_DV)] += o
        return carry

    lax.fori_loop(0, GDN_NCH, recur, 0)


def _gdn_chunk(q, k, v, gb):
    wide = GDN_HP * GDN_DK
    nchain = 2 * GDN_HP
    blk = pl.BlockSpec((1, T, wide), lambda b, hp: (b, 0, hp))
    return pl.pallas_call(
        _gdn_chunk_kernel,
        grid=(B, GDN_HEADS // GDN_HP),
        in_specs=[blk, blk, blk, pl.BlockSpec((1, T, 128), lambda b, hp: (b, 0, 0))],
        out_specs=blk,
        out_shape=jax.ShapeDtypeStruct((B, T, D), f32),
        scratch_shapes=[
            pltpu.VMEM((nchain, GDN_DK, GDN_DV), f32),
            pltpu.VMEM((nchain, GDN_NCH, 2 * GDN_CHUNK, GDN_DK), bf16),
            pltpu.VMEM((nchain, GDN_NCH, GDN_CHUNK, GDN_DK), bf16),
            pltpu.VMEM((nchain, GDN_NCH, GDN_CHUNK, GDN_DV), bf16),
            pltpu.VMEM((nchain, GDN_NCH, GDN_CHUNK, GDN_CHUNK), bf16),
            pltpu.VMEM((nchain, GDN_NCH, 8, 128), f32),
        ],
        compiler_params=pltpu.CompilerParams(
            dimension_semantics=("arbitrary", "arbitrary"), vmem_limit_bytes=V7X_VMEM_LIMIT),
        name="gdn_chunk",
    )(q, k, v, gb)


def _gdn_out_kernel(o_ref, z_ref, ng_ref, w_ref, y_ref):
    parts = []
    for h in range(GDN_HEADS):
        cols = slice(h * GDN_DV, (h + 1) * GDN_DV)
        oh = o_ref[0, :, cols]
        zz = z_ref[0, :, cols].astype(f32)
        yn = oh * lax.rsqrt(jnp.mean(oh * oh, axis=1, keepdims=True) + EPS) * ng_ref[...]
        parts.append((yn * (zz * jax.nn.sigmoid(zz))).astype(bf16))
    y_ref[0] = jnp.dot(jnp.concatenate(parts, axis=1), w_ref[...], preferred_element_type=f32).astype(bf16)


def _gdn_out(o, z, norm_g, w_out):
    tile = pl.BlockSpec((1, GDN_TM, D), lambda b, j: (b, j, 0))
    return pl.pallas_call(
        _gdn_out_kernel,
        grid=(B, GDN_NT),
        in_specs=[tile, tile, pl.BlockSpec((1, GDN_DV), lambda b, j: (0, 0)),
                  pl.BlockSpec((D, D), lambda b, j: (0, 0))],
        out_specs=tile,
        out_shape=jax.ShapeDtypeStruct((B, T, D), bf16),
        compiler_params=pltpu.CompilerParams(
            dimension_semantics=("arbitrary", "arbitrary"), vmem_limit_bytes=V7X_VMEM_LIMIT),
        name="gdn_out",
    )(o, z, norm_g, w_out)


def _token_order(y):
    lat = y[:, :SEQ].reshape(B, GRID_W, GRID_ROWS, D).transpose(0, 2, 1, 3).reshape(B, SEQ, D)
    return jnp.concatenate([lat, y[:, SEQ:]], axis=1)


def _gdn_mixer(h, w_in, conv_w, a_log, dt_bias, norm_g, w_out):
    q, k, v, z, gb = _gdn_proj(h, w_in, conv_w, a_log, dt_bias)
    o = _gdn_chunk(q, k, v, gb)
    return _token_order(_gdn_out(o, z, norm_g[None], w_out.astype(bf16)))


def kernel(x, c, ctx, c_ctx, mod_w, mod_b, norm_g, ffn_w13, ffn_w2, s5_a_re, s5_a_im, s5_log_dt,
           s5_b_re, s5_b_im, s5_c_re, s5_c_im, s5_d, s5_glu_w, s5_glu_b, gdn_w_in, gdn_conv_w,
           gdn_a_log, gdn_dt_bias, gdn_norm_g, gdn_w_out, final_g):
    tok = jnp.concatenate([x, ctx], axis=1)
    mods = _modulation(c, c_ctx, mod_w, mod_b).reshape(DEPTH, MOD_ROWS, N_MOD, D)
    w13 = ffn_w13.astype(bf16)
    w2 = ffn_w2.astype(bf16)
    for i in range(DEPTH):
        m = mods[i]
        tok, h = _ffn(tok, m[:, 0:3], norm_g[i, 0][None], w13, w2, i, 0,
                      next_norm=(m[:, 3:6], norm_g[i, 1][None]))
        j = i // 2
        if i % 2 == 0:
            y = _s5_unflatten(_s5_core(_s5_flatten(h), s5_a_re[j], s5_a_im[j], s5_log_dt[j], s5_b_re[j],
                                       s5_b_im[j], s5_c_re[j], s5_c_im[j], s5_d[j]))
            tok = _glu_residual(tok, y, m[:, 3:6], s5_glu_w[j].astype(bf16), s5_glu_b[j][None])
            tok = _ffn(tok, m[:, 6:9], norm_g[i, 2][None], w13, w2, i, 1)
        else:
            y = _gdn_mixer(h, gdn_w_in[j], gdn_conv_w[j], gdn_a_log[j], gdn_dt_bias[j], gdn_norm_g[j],
                           gdn_w_out[j])
            tok = _ffn(tok, m[:, 6:9], norm_g[i, 2][None], w13, w2, i, 1, mixer_out=y, m3_mixer=m[:, 3:6])
    return _final_norm(tok, final_g[None])
```

```python
import functools

import jax
import jax.numpy as jnp
from jax import lax
from jax.experimental import pallas as pl
from jax.experimental.pallas import tpu as pltpu

D = 1024
B = 4
SEQ = 4096
CTX = 256
T = SEQ + CTX
DEPTH = 4
GRID_W = 64
N_MOD = 9
D_FF = 2816
S5_GROUP = 16
S5_GROUPS = D // S5_GROUP
S5_STATE = 64
GDN_HEADS = 8
GDN_DK = D // GDN_HEADS
GDN_DV = D // GDN_HEADS
GDN_CONV = 5
GDN_CHUNK = 128
EPS = 1e-6

MOD_ROWS = 8
V7X_VMEM_LIMIT = 56 * 1024 * 1024

f32 = jnp.float32
bf16 = jnp.bfloat16


MOD_TN = 2304


def _mod_kernel(c_ref, w_ref, b_ref, o_ref):
    c = c_ref[...]
    s = c * jax.nn.sigmoid(c)
    o_ref[0] = jnp.dot(s, w_ref[0], preferred_element_type=f32,
                       precision=lax.Precision.HIGHEST) + b_ref[0]


def _modulation(c, c_ctx, mod_w, mod_b):
    cc = jnp.zeros((MOD_ROWS, D), f32).at[:B].set(c).at[B].set(c_ctx)
    n = N_MOD * D
    return pl.pallas_call(
        _mod_kernel,
        grid=(DEPTH, n // MOD_TN),
        in_specs=[
            pl.BlockSpec((MOD_ROWS, D), lambda i, j: (0, 0)),
            pl.BlockSpec((1, D, MOD_TN), lambda i, j: (i, 0, j)),
            pl.BlockSpec((1, 1, MOD_TN), lambda i, j: (i, 0, j)),
        ],
        out_specs=pl.BlockSpec((1, MOD_ROWS, MOD_TN), lambda i, j: (i, 0, j)),
        out_shape=jax.ShapeDtypeStruct((DEPTH, MOD_ROWS, n), f32),
        compiler_params=pltpu.CompilerParams(
            dimension_semantics=("arbitrary", "arbitrary"), vmem_limit_bytes=V7X_VMEM_LIMIT),
        name="modulation",
    )(cc, mod_w, mod_b.reshape(DEPTH, 1, n))


def _row_modulation(m_ref, b, t, tm):
    rows = t * tm + lax.broadcasted_iota(jnp.int32, (tm, 1), 0)
    is_ctx = rows >= SEQ
    mx = m_ref[b]
    mc = m_ref[B]
    return tuple(jnp.where(is_ctx, mc[j:j + 1], mx[j:j + 1]) for j in range(3))


def _ada_norm(x, g, shift, scale):
    ms = jnp.mean(x * x, axis=-1, keepdims=True)
    return x * lax.rsqrt(ms + EPS) * g * (1.0 + scale) + shift


FFN_TM = 1088
FFN_CHUNK = 256


def _ffn_kernel(*refs, with_mixer_out, with_next_norm):
    refs = list(refs)
    if with_mixer_out:
        mm_ref, y_ref = refs[:2]
        refs = refs[2:]
    m_ref, g_ref, x_ref, w13_ref, w2_ref = refs[:5]
    refs = refs[5:]
    if with_next_norm:
        mn_ref, gn_ref, o_ref, h_ref = refs
    else:
        (o_ref,) = refs
    b = pl.program_id(0)
    t = pl.program_id(1)
    x = x_ref[0]
    if with_mixer_out:
        _, _, gate_mix = _row_modulation(mm_ref, b, t, FFN_TM)
        x = x + gate_mix * y_ref[0].astype(f32)
    shift, scale, gate = _row_modulation(m_ref, b, t, FFN_TM)
    h = _ada_norm(x, g_ref[...], shift, scale).astype(bf16)
    acc = jnp.zeros((FFN_TM, D), f32)
    for j in range(D_FF // FFN_CHUNK):
        lo = j * FFN_CHUNK
        a = jnp.dot(h, w13_ref[:, lo:lo + FFN_CHUNK], preferred_element_type=f32)
        u = jnp.dot(h, w13_ref[:, D_FF + lo:D_FF + lo + FFN_CHUNK], preferred_element_type=f32)
        hid = (a * jax.nn.sigmoid(a) * u).astype(bf16)
        acc = acc + jnp.dot(hid, w2_ref[lo:lo + FFN_CHUNK, :], preferred_element_type=f32)
    out = x + 0.5 * gate * acc
    o_ref[0] = out
    if with_next_norm:
        shift_n, scale_n, _ = _row_modulation(mn_ref, b, t, FFN_TM)
        h_ref[0] = _ada_norm(out, gn_ref[...], shift_n, scale_n).astype(bf16)


def _ffn(tok, m3, g, w13, w2, layer, which, mixer_out=None, m3_mixer=None, next_norm=None):
    mod = pl.BlockSpec((MOD_ROWS, 3, D), lambda b, t: (0, 0, 0))
    gain = pl.BlockSpec((1, D), lambda b, t: (0, 0))
    tile = pl.BlockSpec((1, FFN_TM, D), lambda b, t: (b, t, 0))
    in_specs = [
        mod, gain, tile,
        pl.BlockSpec((None, None, D, 2 * D_FF), lambda b, t: (layer, which, 0, 0), pipeline_mode=pl.Buffered(1)),
        pl.BlockSpec((None, None, D_FF, D), lambda b, t: (layer, which, 0, 0), pipeline_mode=pl.Buffered(1)),
    ]
    args = (m3, g, tok, w13, w2)
    out_specs = tile
    out_shape = jax.ShapeDtypeStruct((B, T, D), f32)
    if mixer_out is not None:
        in_specs = [mod, tile] + in_specs
        args = (m3_mixer, mixer_out) + args
    if next_norm is not None:
        in_specs = in_specs + [mod, gain]
        args = args + tuple(next_norm)
        out_specs = [tile, tile]
        out_shape = [out_shape, jax.ShapeDtypeStruct((B, T, D), bf16)]
    return pl.pallas_call(
        functools.partial(_ffn_kernel, with_mixer_out=mixer_out is not None,
                          with_next_norm=next_norm is not None),
        grid=(B, T // FFN_TM),
        in_specs=in_specs,
        out_specs=out_specs,
        out_shape=out_shape,
        compiler_params=pltpu.CompilerParams(
            dimension_semantics=("arbitrary", "arbitrary"), vmem_limit_bytes=V7X_VMEM_LIMIT),
        name="ffn",
    )(*args)


FINAL_TM = 512


def _final_kernel(g_ref, x_ref, o_ref):
    x = x_ref[0]
    ms = jnp.mean(x * x, axis=-1, keepdims=True)
    o_ref[0] = x * lax.rsqrt(ms + EPS) * g_ref[...]


def _final_norm(tok, g):
    return pl.pallas_call(
        _final_kernel,
        grid=(B, SEQ // FINAL_TM),
        in_specs=[
            pl.BlockSpec((1, D), lambda b, t: (0, 0)),
            pl.BlockSpec((1, FINAL_TM, D), lambda b, t: (b, t, 0)),
        ],
        out_specs=pl.BlockSpec((1, FINAL_TM, D), lambda b, t: (b, t, 0)),
        out_shape=jax.ShapeDtypeStruct((B, SEQ, D), f32),
        compiler_params=pltpu.CompilerParams(dimension_semantics=("arbitrary", "arbitrary")),
        name="final_norm",
    )(g, tok)


S5_L = 64
S5_NCH = T // S5_L
S5_CCH = CTX // S5_L
S5_ROWS = S5_NCH * B
S5_W = S5_L * S5_GROUP
S5_BASE = 8


def _cmul(ar, ai, br, bi):
    return ar * br - ai * bi, ar * bi + ai * br


def _dot_nt(a, b):
    return lax.dot_general(a, b, (((1,), (1,)), ((), ())), preferred_element_type=f32)


def _s5_kernel(u_ref, are_ref, aim_ref, ldt_ref, btre_ref, btim_ref, cre_ref, cim_ref, dt_ref, y_ref,
               m_ref, s_ref, hf_ref, hb_ref):
    P = S5_STATE
    lane = lax.broadcasted_iota(jnp.int32, (1, 2 * P), 1)
    is_f = lane < P
    mf = is_f.astype(f32)
    mb = 1.0 - mf

    a_re = are_ref[0]
    a_im = aim_ref[0]
    dt = jnp.exp(ldt_ref[0])
    mag = jnp.exp(dt * a_re)
    ang = dt * a_im
    l_re = mag * jnp.cos(ang)
    l_im = mag * jnp.sin(ang)
    den = a_re * a_re + a_im * a_im
    n_re = l_re - 1.0
    f_re = (n_re * a_re + l_im * a_im) / den
    f_im = (l_im * a_re - n_re * a_im) / den

    c_re = cre_ref[0]
    c_im = cim_ref[0]
    a_r, a_i = _cmul(f_re, f_im, btre_ref[0], btim_ref[0])
    r_r, r_i = _cmul(l_re, l_im, c_re, c_im)
    p_re, p_im = l_re, l_im
    for _ in range(6):
        fm_re = jnp.where(is_f, p_re, 1.0)
        fm_im = jnp.where(is_f, p_im, 0.0)
        bm_re = jnp.where(is_f, 1.0, p_re)
        bm_im = jnp.where(is_f, 0.0, p_im)
        at_r, at_i = _cmul(fm_re, fm_im, a_r, a_i)
        ab_r, ab_i = _cmul(bm_re, bm_im, a_r, a_i)
        rt_r, rt_i = _cmul(bm_re, bm_im, r_r, r_i)
        rb_r, rb_i = _cmul(fm_re, fm_im, r_r, r_i)
        a_r = jnp.concatenate([at_r, ab_r], axis=0)
        a_i = jnp.concatenate([at_i, ab_i], axis=0)
        r_r = jnp.concatenate([rt_r, rb_r], axis=0)
        r_i = jnp.concatenate([rt_i, rb_i], axis=0)
        p_re, p_im = _cmul(p_re, p_im, p_re, p_im)
    acat = jnp.concatenate([a_r, a_i], axis=1).astype(bf16)
    rcat = jnp.concatenate([r_r, -r_i], axis=1)
    mf2 = jnp.concatenate([mf, mf], axis=1)
    rtf = (rcat * mf2).astype(bf16)
    rtb = (rcat * (1.0 - mf2)).astype(bf16)
    rcat = rcat.astype(bf16)

    nb = S5_BASE
    ccat = jnp.concatenate([c_re, -c_im], axis=1)
    ctf = jnp.concatenate([(ccat * mf2).astype(bf16)] * nb, axis=0)
    ctb = jnp.concatenate([(ccat * (1.0 - mf2)).astype(bf16)] * nb, axis=0)
    bw = nb * S5_GROUP
    kf_all = _dot_nt(acat[S5_W - bw:], ctf)
    kb_all = _dot_nt(acat[:bw], ctb)
    lane_t0 = lax.broadcasted_iota(jnp.int32, (S5_GROUP, bw), 1) // S5_GROUP
    lane_i = lax.broadcasted_iota(jnp.int32, (S5_GROUP, bw), 1)
    row_i = lax.broadcasted_iota(jnp.int32, (S5_GROUP, bw), 0)
    dtile = dt_ref[0]
    blocks = []
    for s0 in range(nb):
        acc = jnp.where(lane_i == s0 * S5_GROUP + row_i, dtile, 0.0)
        for lag in range(nb - s0):
            acc = acc + jnp.where(lane_t0 == s0 + lag, kf_all[(nb - 1 - lag) * S5_GROUP:(nb - lag) * S5_GROUP], 0.0)
        for lag in range(s0 + 1):
            acc = acc + jnp.where(lane_t0 == s0 - lag, kb_all[lag * S5_GROUP:(lag + 1) * S5_GROUP], 0.0)
        blocks.append(acc)
    blk0 = jnp.concatenate(blocks, axis=0).astype(bf16)
    for i in range(S5_L // nb):
        m_ref[i * bw:(i + 1) * bw, i * bw:(i + 1) * bw] = blk0
    n = nb
    while n < S5_L:
        w = n * S5_GROUP
        xn = _dot_nt(acat[S5_W - w:], rtf[:w]).astype(bf16)
        yn = _dot_nt(acat[:w], rtb[S5_W - w:]).astype(bf16)
        for j in range(S5_L // (2 * n)):
            o = 2 * j * w
            m_ref[o:o + w, o + w:o + 2 * w] = xn
            m_ref[o + w:o + 2 * w, o:o + w] = yn
        n *= 2

    u = u_ref[0]
    y = jnp.dot(u, m_ref[...], preferred_element_type=f32)
    s_ref[...] = jnp.dot(u, acat, preferred_element_type=f32)

    h_re = jnp.zeros((B, 2 * P), f32)
    h_im = jnp.zeros((B, 2 * P), f32)
    for j in range(S5_NCH):
        lat = S5_NCH - S5_CCH
        kf = lat + j if j < S5_CCH else j - S5_CCH
        kb = lat + S5_CCH - 1 - j if j < S5_CCH else S5_NCH - 1 - j
        hf_ref[kf * B:(kf + 1) * B, :] = jnp.concatenate([h_re, h_im], axis=1)
        hb_ref[kb * B:(kb + 1) * B, :] = jnp.concatenate([h_re, h_im], axis=1)
        s_re = jnp.where(is_f, s_ref[kf * B:(kf + 1) * B, 0:2 * P], s_ref[kb * B:(kb + 1) * B, 0:2 * P])
        s_im = jnp.where(is_f, s_ref[kf * B:(kf + 1) * B, 2 * P:4 * P], s_ref[kb * B:(kb + 1) * B, 2 * P:4 * P])
        h_re, h_im = (p_re * h_re - p_im * h_im + s_re, p_re * h_im + p_im * h_re + s_im)
    hcat = jnp.where(mf2 > 0.5, hf_ref[...], hb_ref[...]).astype(bf16)
    y_ref[0] = (y + _dot_nt(hcat, rcat)).astype(bf16)


def _s5_core(uflat, a_re, a_im, log_dt, b_re, b_im, c_re, c_im, d):
    G = S5_GROUPS

    def pack(t):
        return jnp.concatenate([t[0], t[1]], axis=-1)

    are = pack(a_re)[:, None, :]
    aim = pack(a_im)[:, None, :]
    ldt = pack(jnp.broadcast_to(log_dt[:, :, None], (2, G, S5_STATE)))[:, None, :]
    btre = pack(jnp.swapaxes(b_re, -1, -2))
    btim = pack(jnp.swapaxes(b_im, -1, -2))
    cre = pack(c_re)
    cim = pack(c_im)
    dtile = jnp.tile(d.reshape(G, 1, S5_GROUP), (1, 1, S5_BASE))
    vec = pl.BlockSpec((1, 1, 2 * S5_STATE), lambda g: (g, 0, 0))
    mat = pl.BlockSpec((1, S5_GROUP, 2 * S5_STATE), lambda g: (g, 0, 0))
    big = pl.BlockSpec((1, S5_ROWS, S5_W), lambda g: (g, 0, 0))
    return pl.pallas_call(
        _s5_kernel,
        grid=(G,),
        in_specs=[big, vec, vec, vec, mat, mat, mat, mat, vec],
        out_specs=big,
        out_shape=jax.ShapeDtypeStruct((G, S5_ROWS, S5_W), bf16),
        scratch_shapes=[
            pltpu.VMEM((S5_W, S5_W), bf16),
            pltpu.VMEM((S5_ROWS, 4 * S5_STATE), f32),
            pltpu.VMEM((S5_ROWS, 4 * S5_STATE), f32),
            pltpu.VMEM((S5_ROWS, 4 * S5_STATE), f32),
        ],
        compiler_params=pltpu.CompilerParams(
            dimension_semantics=("arbitrary",), vmem_limit_bytes=V7X_VMEM_LIMIT),
        name="s5_core",
    )(uflat, are, aim, ldt, btre, btim, cre, cim, dtile)


SEG_TM = 1088
SEG_UNROLL = 8


def _seg_transpose_kernel(x_ref, o_ref):
    a = lax.broadcasted_iota(jnp.int32, (8, D), 0)
    b = (lax.broadcasted_iota(jnp.int32, (8, D), 1) // S5_GROUP) % 8

    def body(i, carry):
        rows = [pl.multiple_of((i * SEG_UNROLL + u) * 8, 8) for u in range(SEG_UNROLL)]
        xs = [x_ref[0, pl.ds(r, 8), :].astype(f32) for r in rows]
        for j in range(3):
            d = 1 << j
            same = ((a >> j) & 1) == ((b >> j) & 1)
            low = ((a >> j) & 1) == 0
            ups = [pltpu.roll(pltpu.roll(x, 8 - d, 0), S5_GROUP * d, 1) for x in xs]
            dns = [pltpu.roll(pltpu.roll(x, d, 0), D - S5_GROUP * d, 1) for x in xs]
            xs = [jnp.where(same, x, jnp.where(low, up, dn)) for x, up, dn in zip(xs, ups, dns)]
        for r, x in zip(rows, xs):
            o_ref[0, pl.ds(r, 8), :] = x.astype(o_ref.dtype)
        return carry

    lax.fori_loop(0, SEG_TM // (8 * SEG_UNROLL), body, 0)


def _seg_transpose(x):
    tile = pl.BlockSpec((1, SEG_TM, D), lambda b, t: (b, t, 0))
    return pl.pallas_call(
        _seg_transpose_kernel,
        grid=(B, T // SEG_TM),
        in_specs=[tile],
        out_specs=tile,
        out_shape=jax.ShapeDtypeStruct((B, T, D), x.dtype),
        compiler_params=pltpu.CompilerParams(dimension_semantics=("arbitrary", "arbitrary")),
        name="seg_transpose",
    )(x)


def _s5_flatten(h):
    hp = _seg_transpose(h).reshape(B, S5_NCH, S5_L // 8, 8, S5_GROUPS // 8, 128)
    return hp.transpose(4, 3, 1, 0, 2, 5).reshape(S5_GROUPS, S5_ROWS, S5_W)


def _s5_unflatten(y):
    yp = y.reshape(S5_GROUPS // 8, 8, S5_NCH, B, S5_L // 8, 128)
    return _seg_transpose(yp.transpose(3, 2, 4, 1, 0, 5).reshape(B, T, D))


GLU_TM = 544


def _glu_kernel(m_ref, y_ref, x_ref, w_ref, b_ref, o_ref):
    b = pl.program_id(0)
    t = pl.program_id(1)
    _, _, gate = _row_modulation(m_ref, b, t, GLU_TM)
    z = jax.nn.gelu(y_ref[0].astype(f32)).astype(bf16)
    p = jnp.dot(z, w_ref[...], preferred_element_type=f32) + b_ref[...]
    o_ref[0] = x_ref[0] + gate * (p[:, :D] * jax.nn.sigmoid(p[:, D:]))


def _glu_residual(tok, y, m3, glu_w, glu_b):
    tile = pl.BlockSpec((1, GLU_TM, D), lambda b, t: (b, t, 0))
    return pl.pallas_call(
        _glu_kernel,
        grid=(B, T // GLU_TM),
        in_specs=[
            pl.BlockSpec((MOD_ROWS, 3, D), lambda b, t: (0, 0, 0)),
            tile, tile,
            pl.BlockSpec((D, 2 * D), lambda b, t: (0, 0)),
            pl.BlockSpec((1, 2 * D), lambda b, t: (0, 0)),
        ],
        out_specs=tile,
        out_shape=jax.ShapeDtypeStruct((B, T, D), f32),
        compiler_params=pltpu.CompilerParams(
            dimension_semantics=("arbitrary", "arbitrary"), vmem_limit_bytes=V7X_VMEM_LIMIT),
        name="glu_residual",
    )(m3, y, tok, glu_w, glu_b)


GDN_TM = 256
GDN_NT = T // GDN_TM
GRID_ROWS = SEQ // GRID_W
GDN_WT = GDN_TM // GRID_ROWS
GDN_CB = 16
GDN_CT = GDN_CB // GDN_WT
assert CTX == GDN_TM and SEQ % GDN_TM == 0 and GRID_W % GDN_CB == 0
GDN_HALO = 16
GDN_QKV = 3 * D
GDN_NPROJ = 4 * D + 128
GDN_NCH = T // GDN_CHUNK
GDN_CCH = CTX // GDN_CHUNK
GDN_HP = 2


def _gdn_proj_kernel(main_ref, prev_ref, next_ref, ctx_ref, w_ref, cw_ref, alog_ref, dtb_ref,
                     q_ref, k_ref, v_ref, z_ref, gb_ref, p_ref, hbuf):
    j = pl.program_id(1)
    pad = GDN_CONV // 2
    lo = GDN_HALO
    hi = GDN_HALO + GDN_TM
    top = GRID_ROWS - GDN_HALO

    for jm in range(GDN_CT):
        @pl.when((j < GDN_NT - 1) & (j % GDN_CT == jm))
        def _(jm=jm):
            w0 = jm * GDN_WT
            for r in range(GDN_WT):
                hbuf[lo + r * GRID_ROWS:lo + (r + 1) * GRID_ROWS, :] = main_ref[0, :, w0 + r, :]
            hbuf[0:lo, :] = main_ref[0, top:, w0 - 1, :] if jm > 0 else prev_ref[0, :, GDN_CB - 1, :]
            hbuf[hi:hi + GDN_HALO, :] = (main_ref[0, 0:GDN_HALO, w0 + GDN_WT, :] if jm < GDN_CT - 1
                                         else next_ref[0, :, 0, :])

    @pl.when(j == GDN_NT - 1)
    def _():
        hbuf[lo:hi, :] = ctx_ref[0].reshape(GDN_TM, D)

    @pl.when((j == 0) | (j == GDN_NT - 1))
    def _():
        hbuf[0:lo, :] = jnp.zeros((GDN_HALO, D), bf16)

    @pl.when(j >= GDN_NT - 2)
    def _():
        hbuf[hi:hi + GDN_HALO, :] = jnp.zeros((GDN_HALO, D), bf16)

    hm = hbuf[lo:hi, :]
    p_ref[...] = jnp.dot(hbuf[...], w_ref[:, 0:GDN_QKV], preferred_element_type=f32)

    for cb in range(GDN_QKV // GDN_DK):
        cols = slice(cb * GDN_DK, (cb + 1) * GDN_DK)
        acc = cw_ref[0:1, cols] * p_ref[lo - pad:hi - pad, cols]
        for kk in range(1, GDN_CONV):
            acc = acc + cw_ref[kk:kk + 1, cols] * p_ref[lo - pad + kk:hi - pad + kk, cols]
        xa = acc * jax.nn.sigmoid(acc)
        if cb < 2 * GDN_HEADS:
            xa = xa * lax.rsqrt(jnp.sum(xa * xa, axis=1, keepdims=True) + EPS)
        if cb < GDN_HEADS:
            q_ref[0, :, cols] = (xa * (GDN_DK ** -0.5)).astype(bf16)
        elif cb < 2 * GDN_HEADS:
            k_ref[0, :, slice((cb - GDN_HEADS) * GDN_DK, (cb - GDN_HEADS + 1) * GDN_DK)] = xa.astype(bf16)
        else:
            v_ref[0, :, slice((cb - 2 * GDN_HEADS) * GDN_DV, (cb - 2 * GDN_HEADS + 1) * GDN_DV)] = xa.astype(bf16)

    z_ref[0] = jnp.dot(hm, w_ref[:, GDN_QKV:4 * D], preferred_element_type=f32).astype(bf16)

    ba = jnp.dot(hm, w_ref[:, 4 * D:GDN_NPROJ], preferred_element_type=f32)
    xs = ba + dtb_ref[...]
    softplus = jnp.maximum(xs, 0.0) + jnp.log(1.0 + jnp.exp(-jnp.abs(xs)))
    g = -jnp.exp(alog_ref[...]) * softplus
    ri = lax.broadcasted_iota(jnp.int32, (GDN_TM, GDN_TM), 0)
    ci = lax.broadcasted_iota(jnp.int32, (GDN_TM, GDN_TM), 1)
    same = (ri // GDN_CHUNK) == (ci // GDN_CHUNK)
    tri_f = (same & (ci <= ri)).astype(f32)
    tri_b = (same & (ci >= ri)).astype(f32)
    gc_f = jnp.dot(tri_f, g, preferred_element_type=f32, precision=lax.Precision.HIGHEST)
    gc_b = jnp.dot(tri_b, g, preferred_element_type=f32, precision=lax.Precision.HIGHEST)
    lane = lax.broadcasted_iota(jnp.int32, (GDN_TM, 128), 1)
    gb_ref[0] = jnp.where(lane < GDN_HEADS, gc_f,
                          jnp.where(lane < 2 * GDN_HEADS, gc_b,
                                    jnp.where(lane < 4 * GDN_HEADS, jax.nn.sigmoid(ba), 0.0)))


def _gdn_proj(h, w_in, conv_w, a_log, dt_bias):
    nh = 2 * GDN_HEADS
    w = jnp.concatenate([w_in[:, :4 * D], w_in[:, 4 * D + nh:], w_in[:, 4 * D:4 * D + nh],
                         jnp.zeros((D, 128 - 2 * nh), f32)], axis=1).astype(bf16)
    alog = jnp.zeros((1, 128), f32).at[0, :nh].set(a_log.reshape(nh))
    dtb = jnp.zeros((1, 128), f32).at[0, :nh].set(dt_bias.reshape(nh))
    tile = pl.BlockSpec((1, GDN_TM, D), lambda b, j: (b, j, 0))
    rows = GDN_TM + 2 * GDN_HALO
    ncb = GRID_W // GDN_CB
    h4 = h.reshape(B, T // GRID_W, GRID_W, D)

    def cblock(j):
        return jnp.minimum(j // GDN_CT, ncb - 1)

    return pl.pallas_call(
        _gdn_proj_kernel,
        grid=(B, GDN_NT),
        in_specs=[
            pl.BlockSpec((1, GRID_ROWS, GDN_CB, D), lambda b, j: (b, 0, cblock(j), 0)),
            pl.BlockSpec((1, GDN_HALO, GDN_CB, D),
                         lambda b, j: (b, GRID_ROWS // GDN_HALO - 1, jnp.maximum(cblock(j) - 1, 0), 0)),
            pl.BlockSpec((1, GDN_HALO, GDN_CB, D), lambda b, j: (b, 0, jnp.minimum(cblock(j) + 1, ncb - 1), 0)),
            pl.BlockSpec((1, CTX // GRID_W, GRID_W, D), lambda b, j: (b, GRID_ROWS // (CTX // GRID_W), 0, 0)),
            pl.BlockSpec((D, GDN_NPROJ), lambda b, j: (0, 0), pipeline_mode=pl.Buffered(1)),
            pl.BlockSpec((GDN_CONV, GDN_QKV), lambda b, j: (0, 0)),
            pl.BlockSpec((1, 128), lambda b, j: (0, 0)),
            pl.BlockSpec((1, 128), lambda b, j: (0, 0)),
        ],
        out_specs=[tile, tile, tile, tile, pl.BlockSpec((1, GDN_TM, 128), lambda b, j: (b, j, 0))],
        out_shape=[jax.ShapeDtypeStruct((B, T, D), bf16)] * 4 + [jax.ShapeDtypeStruct((B, T, 128), f32)],
        scratch_shapes=[pltpu.VMEM((rows, GDN_QKV), f32), pltpu.VMEM((rows, D), bf16)],
        compiler_params=pltpu.CompilerParams(
            dimension_semantics=("arbitrary", "arbitrary"), vmem_limit_bytes=V7X_VMEM_LIMIT),
        name="gdn_proj",
    )(h4, h4, h4, h4, w, conv_w, alog, dtb)


GDN_PA = 2
GDN_INV_BASE = 8


def _gdn_chunk_kernel(q_ref, k_ref, v_ref, gb_ref, o_ref, s_ref, wq_sc, kd_sc, u_sc, at_sc, gl_sc):
    hp = pl.program_id(1)
    C = GDN_CHUNK
    nchain = 2 * GDN_HP
    o_ref[...] = jnp.zeros_like(o_ref)
    s_ref[...] = jnp.zeros_like(s_ref)
    ri = lax.broadcasted_iota(jnp.int32, (C, C), 0)
    ci = lax.broadcasted_iota(jnp.int32, (C, C), 1)
    eye = (ri == ci).astype(f32)
    lane = lax.broadcasted_iota(jnp.int32, (C, 128), 1)
    ones3 = jnp.ones((C, 3 * 128), bf16)
    masks = ((ci <= ri, ci < ri, C - 1), (ci >= ri, ci > ri, 0))
    diag_blk = (ri // GDN_INV_BASE) == (ci // GDN_INV_BASE)
    merge_masks = []
    n = GDN_INV_BASE
    while n < C:
        merge_masks.append(((ri // (2 * n)) == (ci // (2 * n))) & ((ri // n) != (ci // n)))
        n *= 2

    def local(m, carry):
        items = [(hl, d, e) for hl in range(GDN_HP) for d in range(2) for e in range(GDN_PA)]
        st = []
        for hl, d, e in items:
            ch = m * GDN_PA + e
            r0 = pl.multiple_of(ch * C, C)
            cols = slice(hl * GDN_DK, (hl + 1) * GDN_DK)
            q = q_ref[0, pl.ds(r0, C), cols]
            k = k_ref[0, pl.ds(r0, C), cols]
            v = v_ref[0, pl.ds(r0, C), cols].astype(f32)
            gbt = gb_ref[0, pl.ds(r0, C), :]
            lg = d * GDN_HEADS + hp * GDN_HP + hl
            gsel = jnp.where(lane == lg, gbt, 0.0)
            gc = jnp.sum(gsel, axis=1, keepdims=True)
            beta = jnp.sum(jnp.where(lane == 2 * GDN_HEADS + lg, gbt, 0.0), axis=1, keepdims=True)
            a1 = gsel.astype(bf16)
            r1 = gsel - a1.astype(f32)
            a2 = r1.astype(bf16)
            a3 = (r1 - a2.astype(f32)).astype(bf16)
            kf = k.astype(f32)
            kb = kf * beta
            st.append(dict(c=hl * 2 + d, d=d, ch=ch, q=q, k=k, v=v, gc=gc, beta=beta, kf=kf, kb=kb,
                           a3=jnp.concatenate([a1, a2, a3], axis=1)))
        for t in st:
            t["gc_row"] = _dot_nt(ones3, t.pop("a3"))
            t["kq"] = _dot_nt(jnp.concatenate([t["kb"].astype(bf16), t["q"]], axis=0), t["k"])
        for t in st:
            incl, strict, last = masks[t["d"]]
            decay = jnp.exp(jnp.where(incl, t["gc"] - t.pop("gc_row"), -jnp.inf))
            kq = t.pop("kq")
            a_mat = jnp.where(strict, kq[:C] * decay, 0.0)
            at_sc[t["c"], t["ch"]] = jnp.where(incl, kq[C:] * decay, 0.0).astype(bf16)
            t["a"] = a_mat.astype(bf16)
            x = -jnp.where(diag_blk, a_mat, 0.0)
            t["r"] = eye + x
            t["xb"] = x.astype(bf16)
        for t in st:
            xb = t.pop("xb")
            t["y"] = jnp.dot(xb, xb, preferred_element_type=f32)
        for t in st:
            yb = t.pop("y").astype(bf16)
            ry = jnp.dot(jnp.concatenate([t["r"].astype(bf16), yb], axis=0), yb, preferred_element_type=f32)
            t["r"] = t["r"] + ry[:C]
            t["y"] = ry[C:]
        for t in st:
            t["r"] = t["r"] + jnp.dot(t["r"].astype(bf16), t.pop("y").astype(bf16), preferred_element_type=f32)
        for off in merge_masks:
            for t in st:
                t["w"] = jnp.dot(jnp.where(off, t["a"], jnp.zeros_like(t["a"])), t["r"].astype(bf16),
                                 preferred_element_type=f32)
            for t in st:
                t["r"] = t["r"] - jnp.dot(t["r"].astype(bf16), t.pop("w").astype(bf16), preferred_element_type=f32)
        for t in st:
            incl, strict, last = masks[t["d"]]
            gc = t["gc"]
            gexp = jnp.exp(gc)
            rhs = jnp.concatenate([t["v"] * t["beta"], t["kb"] * gexp], axis=1)
            uw = rhs + jnp.dot((t["r"] - eye).astype(bf16), rhs.astype(bf16), preferred_element_type=f32)
            gc_last = gc[last:last + 1]
            c, ch = t["c"], t["ch"]
            u_sc[c, ch] = uw[:, :GDN_DV].astype(bf16)
            wq_sc[c, ch, 0:C, :] = uw[:, GDN_DV:].astype(bf16)
            wq_sc[c, ch, C:2 * C, :] = (t["q"].astype(f32) * gexp).astype(bf16)
            kd_sc[c, ch] = (t["kf"] * jnp.exp(gc_last - gc)).astype(bf16)
            gl_sc[c, ch] = jnp.broadcast_to(jnp.exp(gc_last), (8, 128))
        return carry

    lax.fori_loop(0, GDN_NCH // GDN_PA, local, 0)

    def recur(i, carry):
        st = []
        for hl in range(GDN_HP):
            for d in range(2):
                if d == 0:
                    ch = jnp.where(i < GDN_CCH, SEQ // C + i, i - GDN_CCH)
                else:
                    ch = jnp.where(i < GDN_CCH, SEQ // C + GDN_CCH - 1 - i, GDN_NCH - 1 - i)
                st.append(dict(c=hl * 2 + d, hl=hl, ch=ch))
        for t in st:
            c, ch = t["c"], t["ch"]
            t["s"] = s_ref[c]
            t["wq"] = jnp.dot(wq_sc[c, ch], t["s"].astype(bf16), preferred_element_type=f32)
        for t in st:
            c, ch = t["c"], t["ch"]
            wq = t.pop("wq")
            v_new = (u_sc[c, ch].astype(f32) - wq[:C]).astype(bf16)
            o = wq[C:] + jnp.dot(at_sc[c, ch], v_new, preferred_element_type=f32)
            s_ref[c] = t.pop("s") * gl_sc[c, ch][0:1, :] + lax.dot_general(
                kd_sc[c, ch], v_new, (((0,), (0,)), ((), ())), preferred_element_type=f32)
            r0 = pl.multiple_of(ch * C, C)
            o_ref[0, pl.ds(r0, C), slice(t["hl"] * GDN_DV, (t["hl"] + 1) * GDN_DV)] += o
        return carry

    lax.fori_loop(0, GDN_NCH, recur, 0)


def _gdn_chunk(q, k, v, gb):
    wide = GDN_HP * GDN_DK
    nchain = 2 * GDN_HP
    blk = pl.BlockSpec((1, T, wide), lambda b, hp: (b, 0, hp))
    return pl.pallas_call(
        _gdn_chunk_kernel,
        grid=(B, GDN_HEADS // GDN_HP),
        in_specs=[blk, blk, blk, pl.BlockSpec((1, T, 128), lambda b, hp: (b, 0, 0))],
        out_specs=blk,
        out_shape=jax.ShapeDtypeStruct((B, T, D), f32),
        scratch_shapes=[
            pltpu.VMEM((nchain, GDN_DK, GDN_DV), f32),
            pltpu.VMEM((nchain, GDN_NCH, 2 * GDN_CHUNK, GDN_DK), bf16),
            pltpu.VMEM((nchain, GDN_NCH, GDN_CHUNK, GDN_DK), bf16),
            pltpu.VMEM((nchain, GDN_NCH, GDN_CHUNK, GDN_DV), bf16),
            pltpu.VMEM((nchain, GDN_NCH, GDN_CHUNK, GDN_CHUNK), bf16),
            pltpu.VMEM((nchain, GDN_NCH, 8, 128), f32),
        ],
        compiler_params=pltpu.CompilerParams(
            dimension_semantics=("arbitrary", "arbitrary"), vmem_limit_bytes=V7X_VMEM_LIMIT),
        name="gdn_chunk",
    )(q, k, v, gb)


def _gdn_out_kernel(o_ref, z_ref, ng_ref, w_ref, ylat_ref, yctx_ref):
    j = pl.program_id(1)
    parts = []
    for h in range(GDN_HEADS):
        cols = slice(h * GDN_DV, (h + 1) * GDN_DV)
        oh = o_ref[0, :, cols]
        zz = z_ref[0, :, cols].astype(f32)
        yn = oh * lax.rsqrt(jnp.mean(oh * oh, axis=1, keepdims=True) + EPS) * ng_ref[...]
        parts.append((yn * (zz * jax.nn.sigmoid(zz))).astype(bf16))
    y = jnp.dot(jnp.concatenate(parts, axis=1), w_ref[...], preferred_element_type=f32).astype(bf16)

    @pl.when(j < GDN_NT - 1)
    def _():
        ylat_ref[0] = y

    @pl.when(j == GDN_NT - 1)
    def _():
        yctx_ref[0] = y


def _gdn_out(o, z, norm_g, w_out):
    tile = pl.BlockSpec((1, GDN_TM, D), lambda b, j: (b, j, 0))
    return pl.pallas_call(
        _gdn_out_kernel,
        grid=(B, GDN_NT),
        in_specs=[tile, tile, pl.BlockSpec((1, GDN_DV), lambda b, j: (0, 0)),
                  pl.BlockSpec((D, D), lambda b, j: (0, 0))],
        out_specs=[pl.BlockSpec((1, GDN_TM, D), lambda b, j: (b, jnp.minimum(j, GDN_NT - 2), 0)),
                   pl.BlockSpec((1, GDN_TM, D), lambda b, j: (b, 0, 0))],
        out_shape=[jax.ShapeDtypeStruct((B, SEQ, D), bf16), jax.ShapeDtypeStruct((B, CTX, D), bf16)],
        compiler_params=pltpu.CompilerParams(
            dimension_semantics=("arbitrary", "arbitrary"), vmem_limit_bytes=V7X_VMEM_LIMIT),
        name="gdn_out",
    )(o, z, norm_g, w_out)


def _token_order(y_lat, y_ctx):
    lat = y_lat.reshape(B, GRID_W, GRID_ROWS, D).transpose(0, 2, 1, 3).reshape(B, SEQ, D)
    return jnp.concatenate([lat, y_ctx], axis=1)


def _gdn_mixer(h, w_in, conv_w, a_log, dt_bias, norm_g, w_out):
    q, k, v, z, gb = _gdn_proj(h, w_in, conv_w, a_log, dt_bias)
    o = _gdn_chunk(q, k, v, gb)
    return _token_order(*_gdn_out(o, z, norm_g[None], w_out.astype(bf16)))


def kernel(x, c, ctx, c_ctx, mod_w, mod_b, norm_g, ffn_w13, ffn_w2, s5_a_re, s5_a_im, s5_log_dt,
           s5_b_re, s5_b_im, s5_c_re, s5_c_im, s5_d, s5_glu_w, s5_glu_b, gdn_w_in, gdn_conv_w,
           gdn_a_log, gdn_dt_bias, gdn_norm_g, gdn_w_out, final_g):
    tok = jnp.concatenate([x, ctx], axis=1)
    mods = _modulation(c, c_ctx, mod_w, mod_b).reshape(DEPTH, MOD_ROWS, N_MOD, D)
    w13 = ffn_w13.astype(bf16)
    w2 = ffn_w2.astype(bf16)
    for i in range(DEPTH):
        m = mods[i]
        tok, h = _ffn(tok, m[:, 0:3], norm_g[i, 0][None], w13, w2, i, 0,
                      next_norm=(m[:, 3:6], norm_g[i, 1][None]))
        j = i // 2
        if i % 2 == 0:
            y = _s5_unflatten(_s5_core(_s5_flatten(h), s5_a_re[j], s5_a_im[j], s5_log_dt[j], s5_b_re[j],
                                       s5_b_im[j], s5_c_re[j], s5_c_im[j], s5_d[j]))
            tok = _glu_residual(tok, y, m[:, 3:6], s5_glu_w[j].astype(bf16), s5_glu_b[j][None])
            tok = _ffn(tok, m[:, 6:9], norm_g[i, 2][None], w13, w2, i, 1)
        else:
            y = _gdn_mixer(h, gdn_w_in[j], gdn_conv_w[j], gdn_a_log[j], gdn_dt_bias[j], gdn_norm_g[j],
                           gdn_w_out[j])
            tok = _ffn(tok, m[:, 6:9], norm_g[i, 2][None], w13, w2, i, 1, mixer_out=y, m3_mixer=m[:, 3:6])
    return _final_norm(tok, final_g[None])
```
